```python
import math
import jax, jax.numpy as jnp
from jax import lax
import numpy as np

D_MODEL = 2048
BATCH = 4
SEQ = 4096
DEPTH = 4
DEC_BATCH = 8
DEC_SEQ = 2048
PAST_LEN = 128

N_HEADS = 16
N_KV_HEADS = 4
HEAD_DIM = 64
Q_WIDTH = N_HEADS * HEAD_DIM
KV_WIDTH = N_KV_HEADS * HEAD_DIM
WINDOW = 128
BLOCK = 128
ROT_DIM = HEAD_DIM // 4
ROPE_THETA = 500000.0
CONV_WIDTH = D_MODEL - Q_WIDTH
CONV_KSIZE = 31
IN_A_WIDTH = Q_WIDTH + 2 * KV_WIDTH + 2 * CONV_WIDTH
MIX_A_WIDTH = Q_WIDTH + CONV_WIDTH
SHORT_KSIZE = 3
N_GROUPS = 4
EXPERTS_PER_GROUP = 8
N_EXPERTS = N_GROUPS * EXPERTS_PER_GROUP
TOP_K = 2
D_EXPERT = 512
PLE_DIM = 256
N_A_LAYERS = (DEPTH + 1) // 2
N_C_LAYERS = DEPTH // 2
ALPHA = (2.0 * DEPTH) ** 0.25
BETA = (8.0 * DEPTH) ** -0.25
LN_EPS = 1e-5
NEG_INF = -1e30

kernel_name = "hybrid_bidir_swa_conformer_shortconv_hmoe"


def layer_norm(x, g, b):
    xf = x.astype(jnp.float32)
    mu = jnp.mean(xf, axis=-1, keepdims=True)
    var = jnp.mean(jnp.square(xf - mu), axis=-1, keepdims=True)
    y = (xf - mu) * lax.rsqrt(var + LN_EPS) * g.astype(jnp.float32) + b.astype(jnp.float32)
    return y.astype(x.dtype)


def rope_partial(x, pos):
    half = ROT_DIM // 2
    inv_freq = ROPE_THETA ** (-(jnp.arange(half, dtype=jnp.float32) * 2.0 / ROT_DIM))
    ang = pos[:, None] * inv_freq[None, :]
    cos = jnp.cos(ang)[None, :, None, :].astype(x.dtype)
    sin = jnp.sin(ang)[None, :, None, :].astype(x.dtype)
    x1, x2, rest = x[..., :half], x[..., half:ROT_DIM], x[..., ROT_DIM:]
    return jnp.concatenate([x1 * cos - x2 * sin, x2 * cos + x1 * sin, rest], axis=-1)


def windowed_attention(q, k, v, sink):
    B, S = q.shape[0], q.shape[1]
    nb = S // BLOCK
    grp = N_HEADS // N_KV_HEADS
    qb = q.reshape(B, nb, BLOCK, N_KV_HEADS, grp, HEAD_DIM)

    def band(t):
        tp = jnp.pad(t, ((0, 0), (BLOCK, BLOCK), (0, 0), (0, 0)))
        tb = tp.reshape(B, nb + 2, BLOCK, N_KV_HEADS, HEAD_DIM)
        return jnp.concatenate([tb[:, :-2], tb[:, 1:-1], tb[:, 2:]], axis=2)

    kb, vb = band(k), band(v)
    s = jnp.einsum('bnqhgd,bnkhd->bnhgqk', qb, kb,
                   preferred_element_type=jnp.float32) * (HEAD_DIM ** -0.5)
    qpos = jnp.arange(S).reshape(nb, BLOCK)
    kpos = (jnp.arange(nb)[:, None] - 1) * BLOCK + jnp.arange(3 * BLOCK)[None, :]
    rel = kpos[:, None, :] - qpos[:, :, None]
    valid = (jnp.abs(rel) <= WINDOW) & (kpos[:, None, :] >= 0) & (kpos[:, None, :] < S)
    s = jnp.where(valid[None, :, None, None], s, NEG_INF)
    sink_l = sink.astype(jnp.float32).reshape(N_KV_HEADS, grp)[None, None, :, :, None, None]
    m = jnp.maximum(jnp.max(s, axis=-1, keepdims=True), sink_l)
    pexp = jnp.exp(s - m)
    denom = jnp.sum(pexp, axis=-1, keepdims=True) + jnp.exp(sink_l - m)
    probs = (pexp / denom).astype(v.dtype)
    o = jnp.einsum('bnhgqk,bnkhd->bnqhgd', probs, vb)
    return o.reshape(B, S, Q_WIDTH)


def depthwise_conv(x, w):
    K = w.shape[0]
    return lax.conv_general_dilated(
        x, w[:, None, :].astype(x.dtype), window_strides=(1,),
        padding=((K // 2, K // 2),), dimension_numbers=('NWC', 'WIO', 'NWC'),
        feature_group_count=x.shape[-1])


def conformer_conv(u, w_dw, b_dw, g, b):
    a, gt = jnp.split(u, 2, axis=-1)
    z = a * jax.nn.sigmoid(gt)
    z = depthwise_conv(z, w_dw) + b_dw.astype(z.dtype)
    return jax.nn.silu(layer_norm(z, g, b))


def hier_moe(h, w_rg, b_rg, w_re, b_re, w_up, w_down):
    B, S, D = h.shape
    T = B * S
    xt = h.reshape(T, D)
    lg = (xt @ w_rg).astype(jnp.float32) + b_rg.astype(jnp.float32)
    pg = jax.nn.softmax(lg, axis=-1)
    g_idx = jnp.argmax(lg, axis=-1)
    g_w = jnp.take_along_axis(pg, g_idx[:, None], axis=-1)
    le = ((xt @ w_re).astype(jnp.float32) + b_re.astype(jnp.float32)).reshape(T, N_GROUPS, EXPERTS_PER_GROUP)
    le_sel = jnp.take_along_axis(le, g_idx[:, None, None], axis=1)[:, 0]
    top_l, top_i = lax.top_k(le_sel, TOP_K)
    w_tok = jax.nn.softmax(top_l, axis=-1) * g_w
    e_ids = (g_idx[:, None] * EXPERTS_PER_GROUP + top_i).reshape(-1)
    order = jnp.argsort(e_ids)
    tok = order // TOP_K
    xs = xt[tok]
    sizes = jnp.bincount(e_ids, length=N_EXPERTS).astype(jnp.int32)
    up = lax.ragged_dot(xs, w_up.astype(xs.dtype), sizes)
    a, b = jnp.split(up, 2, axis=-1)
    hid = jax.nn.silu(a) * b
    out = lax.ragged_dot(hid, w_down.astype(hid.dtype), sizes)
    out = out * w_tok.reshape(-1)[order][:, None].astype(out.dtype)
    y = jax.ops.segment_sum(out, tok, num_segments=T)
    return y.reshape(B, S, D)


def trunk(x, p, w_in_a, sink, conv_dw_a, conv_b_a, conv_ln_g, conv_ln_b, w_out_a,
          w_in_c, conv_w_c, w_out_c, ln1_g, ln1_b, ln2_g, ln2_b,
          w_router_group, b_router_group, w_router_expert, b_router_expert,
          w_up, w_down, w_ple, w_ple_gate, b_ple_gate):
    B, S = x.shape[0], x.shape[1]
    pos = jnp.arange(S, dtype=jnp.float32)
    for i in range(DEPTH):
        j = i // 2
        if i % 2 == 0:
            u = x @ w_in_a[j]
            q, k, v, c = jnp.split(u, [Q_WIDTH, Q_WIDTH + KV_WIDTH, Q_WIDTH + 2 * KV_WIDTH], axis=-1)
            q = rope_partial(q.reshape(B, S, N_HEADS, HEAD_DIM), pos)
            k = rope_partial(k.reshape(B, S, N_KV_HEADS, HEAD_DIM), pos)
            v = v.reshape(B, S, N_KV_HEADS, HEAD_DIM)
            att = windowed_attention(q, k, v, sink[j])
            cv = conformer_conv(c, conv_dw_a[j], conv_b_a[j], conv_ln_g[j], conv_ln_b[j])
            mix = jnp.concatenate([att, cv], axis=-1) @ w_out_a[j]
        else:
            bg, cg, xv = jnp.split(x @ w_in_c[j], 3, axis=-1)
            mix = (bg * depthwise_conv(cg * xv, conv_w_c[j])) @ w_out_c[j]
        h = layer_norm(ALPHA * x + mix, ln1_g[i], ln1_b[i])
        moe = hier_moe(h, w_router_group[i], b_router_group[i], w_router_expert[i],
                       b_router_expert[i], w_up[i], w_down[i])
        h = layer_norm(ALPHA * h + moe, ln2_g[i], ln2_b[i])
        gate = jax.nn.sigmoid(h @ w_ple_gate[i] + b_ple_gate[i])
        x = h + gate * (p[i] @ w_ple[i])
    return x


def setup_inputs(seed: int = 0) -> dict:
    key = jax.random.key(seed)
    ks = jax.random.split(key, 32)
    f32 = jnp.float32

    def nrm(k, shape, scale):
        return jax.random.normal(k, shape, f32) * scale

    D = D_MODEL
    v_scale = jnp.ones((IN_A_WIDTH,), f32).at[Q_WIDTH + KV_WIDTH:Q_WIDTH + 2 * KV_WIDTH].set(BETA)
    return {
        "x_prompt": nrm(ks[0], (BATCH, SEQ, D), 1.0),
        "x_sample": nrm(ks[1], (DEC_BATCH, DEC_SEQ, D), 1.0),
        "p_prompt": nrm(ks[2], (DEPTH, BATCH, SEQ, PLE_DIM), 1.0),
        "p_sample": nrm(ks[3], (DEPTH, DEC_BATCH, DEC_SEQ, PLE_DIM), 1.0),
        "w_in_a": nrm(ks[4], (N_A_LAYERS, D, IN_A_WIDTH), D ** -0.5) * v_scale,
        "sink": nrm(ks[5], (N_A_LAYERS, N_HEADS), 0.5),
        "conv_dw_a": nrm(ks[6], (N_A_LAYERS, CONV_KSIZE, CONV_WIDTH), CONV_KSIZE ** -0.5),
        "conv_b_a": nrm(ks[7], (N_A_LAYERS, CONV_WIDTH), 0.02),
        "conv_ln_g": 1.0 + nrm(ks[8], (N_A_LAYERS, CONV_WIDTH), 0.02),
        "conv_ln_b": nrm(ks[9], (N_A_LAYERS, CONV_WIDTH), 0.02),
        "w_out_a": nrm(ks[10], (N_A_LAYERS, MIX_A_WIDTH, D), BETA * MIX_A_WIDTH ** -0.5),
        "w_in_c": nrm(ks[11], (N_C_LAYERS, D, 3 * D), D ** -0.5),
        "conv_w_c": nrm(ks[12], (N_C_LAYERS, SHORT_KSIZE, D), SHORT_KSIZE ** -0.5),
        "w_out_c": nrm(ks[13], (N_C_LAYERS, D, D), BETA * D ** -0.5),
        "ln1_g": 1.0 + nrm(ks[14], (DEPTH, D), 0.02),
        "ln1_b": nrm(ks[15], (DEPTH, D), 0.02),
        "ln2_g": 1.0 + nrm(ks[16], (DEPTH, D), 0.02),
        "ln2_b": nrm(ks[17], (DEPTH, D), 0.02),
        "w_router_group": nrm(ks[18], (DEPTH, D, N_GROUPS), D ** -0.5),
        "b_router_group": nrm(ks[19], (DEPTH, N_GROUPS), 0.01),
        "w_router_expert": nrm(ks[20], (DEPTH, D, N_EXPERTS), D ** -0.5),
        "b_router_expert": nrm(ks[21], (DEPTH, N_EXPERTS), 0.01),
        "w_up": nrm(ks[22], (DEPTH, N_EXPERTS, D, 2 * D_EXPERT), BETA * D ** -0.5),
        "w_down": nrm(ks[23], (DEPTH, N_EXPERTS, D_EXPERT, D), BETA * D_EXPERT ** -0.5),
        "w_ple": nrm(ks[24], (DEPTH, PLE_DIM, D), PLE_DIM ** -0.5),
        "w_ple_gate": nrm(ks[25], (DEPTH, D, D), D ** -0.5),
        "b_ple_gate": nrm(ks[26], (DEPTH, D), 0.02),
    }


def reference(x_prompt, x_sample, p_prompt, p_sample, w_in_a, sink, conv_dw_a, conv_b_a,
              conv_ln_g, conv_ln_b, w_out_a, w_in_c, conv_w_c, w_out_c, ln1_g, ln1_b,
              ln2_g, ln2_b, w_router_group, b_router_group, w_router_expert, b_router_expert,
              w_up, w_down, w_ple, w_ple_gate, b_ple_gate):
    y_prompt = trunk(x_prompt, p_prompt, w_in_a, sink, conv_dw_a, conv_b_a, conv_ln_g, conv_ln_b,
                     w_out_a, w_in_c, conv_w_c, w_out_c, ln1_g, ln1_b, ln2_g, ln2_b,
                     w_router_group, b_router_group, w_router_expert, b_router_expert,
                     w_up, w_down, w_ple, w_ple_gate, b_ple_gate)
    y_sample = trunk(x_sample, p_sample, w_in_a, sink, conv_dw_a, conv_b_a, conv_ln_g, conv_ln_b,
                     w_out_a, w_in_c, conv_w_c, w_out_c, ln1_g, ln1_b, ln2_g, ln2_b,
                     w_router_group, b_router_group, w_router_expert, b_router_expert,
                     w_up, w_down, w_ple, w_ple_gate, b_ple_gate)
    return (y_prompt, y_sample)
```

```python
import functools
import math

import jax
import jax.numpy as jnp
from jax import lax
from jax.experimental import pallas as pl
from jax.experimental.pallas import tpu as pltpu

F32 = jnp.float32
BF16 = jnp.bfloat16

N_HEADS = 16
N_KV_HEADS = 4
HEAD_DIM = 64
Q_WIDTH = N_HEADS * HEAD_DIM
KV_WIDTH = N_KV_HEADS * HEAD_DIM
BLOCK = 128
ROT_DIM = HEAD_DIM // 4
ROPE_THETA = 500000.0
CONV_WIDTH = 1024
CONV_KSIZE = 31
CONV_HALO = 16
SHORT_HALO = 8
N_GROUPS = 4
EXPERTS_PER_GROUP = 8
N_EXPERTS = N_GROUPS * EXPERTS_PER_GROUP
D_EXPERT = 512
LN_EPS = 1e-5
NEG_INF = -1e30
LANES = 128

MOE_TILE = 256
VMEM_LIMIT = 56 * 1024 * 1024


def _params(n_axes=1):
    return pltpu.CompilerParams(dimension_semantics=("arbitrary",) * n_axes,
                                vmem_limit_bytes=VMEM_LIMIT)


def _resident(shape):
    nd = len(shape)
    return pl.BlockSpec(shape, lambda *_: (0,) * nd, pipeline_mode=pl.Buffered(1))


def _dot(a, b):
    return jnp.dot(a, b, preferred_element_type=F32)


def _layer_norm(y, g, b):
    mu = jnp.mean(y, axis=-1, keepdims=True)
    d = y - mu
    var = jnp.mean(d * d, axis=-1, keepdims=True)
    return d * lax.rsqrt(var + LN_EPS) * g + b


def _seq_offset(r0, tp, sp, ss):
    in_p = r0 < tp
    off = jnp.where(in_p, r0 % sp, (r0 - tp) % ss)
    return off, jnp.where(in_p, sp, ss)


def _inproj_a_kernel(x_ref, w_ref, rc_ref, ra_ref, rb_ref, q_ref, k_ref, v_ref, z_ref):
    xb = x_ref[...].astype(BF16)
    rc, ra, rb = rc_ref[...], ra_ref[...], rb_ref[...]

    def rope(u):
        outs = []
        for j in range(u.shape[1] // LANES):
            t = u[:, LANES * j:LANES * (j + 1)]
            half = ROT_DIM // 2
            outs.append(t * rc + pltpu.roll(t, LANES - half, 1) * ra + pltpu.roll(t, half, 1) * rb)
        return jnp.concatenate(outs, axis=1)

    cw = 512
    for c in range(Q_WIDTH // cw):
        u = _dot(xb, w_ref[:, cw * c:cw * (c + 1)])
        q_ref[:, cw * c:cw * (c + 1)] = (rope(u) * (HEAD_DIM ** -0.5)).astype(BF16)
    u = _dot(xb, w_ref[:, Q_WIDTH:Q_WIDTH + 2 * KV_WIDTH])
    k_ref[...] = rope(u[:, :KV_WIDTH]).astype(BF16)
    v_ref[...] = u[:, KV_WIDTH:].astype(BF16)
    base = Q_WIDTH + 2 * KV_WIDTH
    for c in range(CONV_WIDTH // cw):
        a = _dot(xb, w_ref[:, base + cw * c:base + cw * (c + 1)])
        g = _dot(xb, w_ref[:, base + CONV_WIDTH + cw * c:base + CONV_WIDTH + cw * (c + 1)])
        z_ref[:, cw * c:cw * (c + 1)] = a * jax.nn.sigmoid(g)


def _rope_tables(smax):
    half = ROT_DIM // 2
    inv_freq = ROPE_THETA ** (-(jnp.arange(half, dtype=F32) * 2.0 / ROT_DIM))
    ang = jnp.arange(smax, dtype=F32)[:, None] * inv_freq[None, :]
    cos, sin = jnp.cos(ang), jnp.sin(ang)
    d = jnp.arange(LANES) % HEAD_DIM
    f = d % half
    rc = jnp.where(d[None, :] < ROT_DIM, cos[:, f], 1.0)
    ra = jnp.where(d[None, :] < half, -sin[:, f], 0.0)
    rb = jnp.where((d[None, :] >= half) & (d[None, :] < ROT_DIM), sin[:, f], 0.0)
    return rc.astype(F32), ra.astype(F32), rb.astype(F32)


def _inproj_a(x, w, tabs, dims):
    tt, d = x.shape
    tp, sp, ss = dims
    tm = min(512, sp, ss)
    n_in = w.shape[1]

    def tab_map(i):
        off, _ = _seq_offset(i * tm, tp, sp, ss)
        return (off // tm, 0)

    row = lambda i: (i, 0)
    tab_spec = pl.BlockSpec((tm, LANES), tab_map)
    return pl.pallas_call(
        _inproj_a_kernel,
        grid=(tt // tm,),
        in_specs=[pl.BlockSpec((tm, d), row), _resident((d, n_in)), tab_spec, tab_spec, tab_spec],
        out_specs=[pl.BlockSpec((tm, Q_WIDTH), row), pl.BlockSpec((tm, KV_WIDTH), row),
                   pl.BlockSpec((tm, KV_WIDTH), row), pl.BlockSpec((tm, CONV_WIDTH), row)],
        out_shape=[jax.ShapeDtypeStruct((tt, Q_WIDTH), BF16), jax.ShapeDtypeStruct((tt, KV_WIDTH), BF16),
                   jax.ShapeDtypeStruct((tt, KV_WIDTH), BF16), jax.ShapeDtypeStruct((tt, CONV_WIDTH), F32)],
        compiler_params=_params(),
        name="inproj_a",
    )(x, w, *tabs)


def _attn_kernel(sink_ref, q_ref, kp_ref, kc_ref, kn_ref, vp_ref, vc_ref, vn_ref, o_ref, *, dims):
    tp, sp, ss = dims
    off, slen = _seq_offset(pl.program_id(0) * BLOCK, tp, sp, ss)
    no_prev = jnp.where(off > 0, 0, BLOCK)
    no_next = jnp.where(off + BLOCK < slen, 0, BLOCK)
    row = lax.broadcasted_iota(jnp.int32, (BLOCK, 3 * BLOCK), 0)
    col = lax.broadcasted_iota(jnp.int32, (BLOCK, 3 * BLOCK), 1)
    jj = col % BLOCK
    blk = col // BLOCK
    valid = ((blk == 1) | ((blk == 0) & (jj >= row + no_prev)) | ((blk == 2) & (jj + no_next <= row)))
    kcat = jnp.concatenate([kp_ref[...], kc_ref[...], kn_ref[...]], axis=0)
    vcat = jnp.concatenate([vp_ref[...], vc_ref[...], vn_ref[...]], axis=0)
    grp = N_HEADS // N_KV_HEADS
    for h in range(N_KV_HEADS):
        kh = kcat[:, HEAD_DIM * h:HEAD_DIM * (h + 1)]
        vh = vcat[:, HEAD_DIM * h:HEAD_DIM * (h + 1)]
        for g in range(grp):
            hh = grp * h + g
            qh = q_ref[:, HEAD_DIM * hh:HEAD_DIM * (hh + 1)]
            s = lax.dot_general(qh, kh, (((1,), (1,)), ((), ())), preferred_element_type=F32)
            s = jnp.where(valid, s, NEG_INF)
            sk = sink_ref[hh]
            m = jnp.maximum(jnp.max(s, axis=-1, keepdims=True), sk)
            p = jnp.exp(s - m)
            denom = jnp.sum(p, axis=-1, keepdims=True) + jnp.exp(sk - m)
            probs = (p / denom).astype(BF16)
            o_ref[:, HEAD_DIM * hh:HEAD_DIM * (hh + 1)] = _dot(probs, vh).astype(BF16)


def _attention(q, k, v, sink, dims):
    tt = q.shape[0]
    nb = tt // BLOCK
    cur = lambda i: (i, 0)
    prv = lambda i: (jnp.maximum(i - 1, 0), 0)
    nxt = lambda i: (jnp.minimum(i + 1, nb - 1), 0)
    kv = lambda m: pl.BlockSpec((BLOCK, KV_WIDTH), m)
    return pl.pallas_call(
        functools.partial(_attn_kernel, dims=dims),
        grid=(nb,),
        in_specs=[pl.BlockSpec(memory_space=pltpu.SMEM), pl.BlockSpec((BLOCK, Q_WIDTH), cur),
                  kv(prv), kv(cur), kv(nxt), kv(prv), kv(cur), kv(nxt)],
        out_specs=pl.BlockSpec((BLOCK, Q_WIDTH), cur),
        out_shape=jax.ShapeDtypeStruct((tt, Q_WIDTH), BF16),
        compiler_params=_params(),
        name="band_attention",
    )(sink, q, k, k, k, v, v, v)


def _conv_a_kernel(z_ref, zp_ref, zn_ref, w_ref, b_ref, g_ref, beta_ref, o_ref, ext_ref, *, dims, ts):
    tp, sp, ss = dims
    off, slen = _seq_offset(pl.program_id(0) * ts, tp, sp, ss)
    keep_prev = (off > 0).astype(F32)
    keep_next = (off + ts < slen).astype(F32)
    ext_ref[0:CONV_HALO, :] = zp_ref[...] * keep_prev
    ext_ref[CONV_HALO:CONV_HALO + ts, :] = z_ref[...]
    ext_ref[CONV_HALO + ts:, :] = zn_ref[...] * keep_next
    rows, lw = 32, 256
    pad = CONV_KSIZE // 2
    for rc in range(ts // rows):
        accs = []
        for lc in range(CONV_WIDTH // lw):
            ls = slice(lw * lc, lw * (lc + 1))
            acc = jnp.zeros((rows, lw), F32)
            for k in range(CONV_KSIZE):
                start = CONV_HALO - pad + k + rows * rc
                acc = acc + ext_ref[start:start + rows, ls] * w_ref[k:k + 1, ls]
            accs.append(acc + b_ref[:, ls])
        y = _layer_norm(jnp.concatenate(accs, axis=1), g_ref[...], beta_ref[...])
        o_ref[rows * rc:rows * (rc + 1), :] = (y * jax.nn.sigmoid(y)).astype(BF16)


def _conv_a(z, w, b, g, beta, dims):
    tt, cw = z.shape
    tp, sp, ss = dims
    ts = min(256, sp, ss)
    hb = ts // CONV_HALO
    nh = tt // CONV_HALO
    vec = lambda a: a.reshape(1, cw)
    return pl.pallas_call(
        functools.partial(_conv_a_kernel, dims=dims, ts=ts),
        grid=(tt // ts,),
        in_specs=[pl.BlockSpec((ts, cw), lambda i: (i, 0)),
                  pl.BlockSpec((CONV_HALO, cw), lambda i: (jnp.maximum(i * hb - 1, 0), 0)),
                  pl.BlockSpec((CONV_HALO, cw), lambda i: (jnp.minimum((i + 1) * hb, nh - 1), 0)),
                  _resident((CONV_KSIZE, cw)), _resident((1, cw)), _resident((1, cw)), _resident((1, cw))],
        out_specs=pl.BlockSpec((ts, cw), lambda i: (i, 0)),
        out_shape=jax.ShapeDtypeStruct((tt, cw), BF16),
        scratch_shapes=[pltpu.VMEM((ts + 2 * CONV_HALO, cw), F32)],
        compiler_params=_params(),
        name="conformer_conv",
    )(z, z, z, w, vec(b), vec(g), vec(beta))


def _norm_and_route(mix, x_ref, g_ref, b_ref, wr_ref, br_ref, h_ref, eid_ref, wt_ref, alpha):
    h = _layer_norm(alpha * x_ref[...] + mix, g_ref[...], b_ref[...])
    h_ref[...] = h
    logits = _dot(h.astype(BF16), wr_ref[...]) + br_ref[...]
    lane = lax.broadcasted_iota(jnp.int32, logits.shape, 1).astype(F32)
    big = jnp.float32(LANES)
    ninf = jnp.float32(-jnp.inf)
    gmask = lane < N_GROUPS
    lg = jnp.where(gmask, logits, ninf)
    gmax = jnp.max(lg, axis=-1, keepdims=True)
    gidx = jnp.min(jnp.where(lg == gmax, lane, big), axis=-1, keepdims=True)
    gsum = jnp.sum(jnp.where(gmask, jnp.exp(logits - gmax), 0.0), axis=-1, keepdims=True)
    lo = N_GROUPS + EXPERTS_PER_GROUP * gidx
    le = jnp.where((lane >= lo) & (lane < lo + EXPERTS_PER_GROUP), logits, ninf)
    t0 = jnp.max(le, axis=-1, keepdims=True)
    i0 = jnp.min(jnp.where(le == t0, lane, big), axis=-1, keepdims=True)
    le1 = jnp.where(lane == i0, ninf, le)
    t1 = jnp.max(le1, axis=-1, keepdims=True)
    i1 = jnp.min(jnp.where(le1 == t1, lane, big), axis=-1, keepdims=True)
    r = jnp.exp(t1 - t0)
    w0 = 1.0 / ((1.0 + r) * gsum)
    w1 = r * w0
    eid_ref[...] = jnp.where(lane == 0, i0 - N_GROUPS, jnp.where(lane == 1, i1 - N_GROUPS, 0.0)).astype(jnp.int32)
    wt_ref[...] = jnp.where(lane == 0, w0, jnp.where(lane == 1, w1, 0.0))


def _outproj_a_kernel(att_ref, cv_ref, x_ref, w_ref, g_ref, b_ref, wr_ref, br_ref,
                      h_ref, eid_ref, wt_ref, *, alpha):
    mix = _dot(att_ref[...], w_ref[0:Q_WIDTH, :]) + _dot(cv_ref[...], w_ref[Q_WIDTH:, :])
    _norm_and_route(mix, x_ref, g_ref, b_ref, wr_ref, br_ref, h_ref, eid_ref, wt_ref, alpha)


def _route_outs(tt, d, tm):
    row = lambda i: (i, 0)
    specs = [pl.BlockSpec((tm, d), row), pl.BlockSpec((tm, LANES), row), pl.BlockSpec((tm, LANES), row)]
    shapes = [jax.ShapeDtypeStruct((tt, d), F32), jax.ShapeDtypeStruct((tt, LANES), jnp.int32),
              jax.ShapeDtypeStruct((tt, LANES), F32)]
    return specs, shapes


def _outproj_a(att, cv, x, w, g, b, wr, br, alpha):
    tt, d = x.shape
    tm = min(512, tt)
    row = lambda i: (i, 0)
    out_specs, out_shape = _route_outs(tt, d, tm)
    return pl.pallas_call(
        functools.partial(_outproj_a_kernel, alpha=alpha),
        grid=(tt // tm,),
        in_specs=[pl.BlockSpec((tm, Q_WIDTH), row), pl.BlockSpec((tm, CONV_WIDTH), row),
                  pl.BlockSpec((tm, d), row), _resident(w.shape), _resident((1, d)), _resident((1, d)),
                  _resident(wr.shape), _resident((1, LANES))],
        out_specs=out_specs, out_shape=out_shape,
        compiler_params=_params(),
        name="outproj_a",
    )(att, cv, x, w, g.reshape(1, d), b.reshape(1, d), wr, br)


def _inproj_c_kernel(x_ref, w_ref, bg_ref, cx_ref):
    xb = x_ref[...].astype(BF16)
    d = x_ref.shape[1]
    cw = 512
    for c in range(d // cw):
        cs = slice(cw * c, cw * (c + 1))
        bg_ref[:, cs] = _dot(xb, w_ref[:, cw * c:cw * (c + 1)])
        cg = _dot(xb, w_ref[:, d + cw * c:d + cw * (c + 1)])
        xv = _dot(xb, w_ref[:, 2 * d + cw * c:2 * d + cw * (c + 1)])
        cx_ref[:, cs] = cg * xv


def _inproj_c(x, w):
    tt, d = x.shape
    tm = min(256, tt)
    row = lambda i: (i, 0)
    return pl.pallas_call(
        _inproj_c_kernel,
        grid=(tt // tm,),
        in_specs=[pl.BlockSpec((tm, d), row), _resident(w.shape)],
        out_specs=[pl.BlockSpec((tm, d), row), pl.BlockSpec((tm, d), row)],
        out_shape=[jax.ShapeDtypeStruct((tt, d), F32), jax.ShapeDtypeStruct((tt, d), F32)],
        compiler_params=_params(),
        name="inproj_c",
    )(x, w)


def _outproj_c_kernel(bg_ref, cx_ref, cp_ref, cn_ref, cw_ref, x_ref, w_ref, g_ref, b_ref, wr_ref, br_ref,
                      h_ref, eid_ref, wt_ref, *, alpha, dims, tm):
    tp, sp, ss = dims
    off, slen = _seq_offset(pl.program_id(0) * tm, tp, sp, ss)
    keep_prev = (off > 0).astype(F32)
    keep_next = (off + tm < slen).astype(F32)
    cx = cx_ref[...]
    first = cp_ref[SHORT_HALO - 1:SHORT_HALO, :] * keep_prev
    last = cn_ref[0:1, :] * keep_next
    rid = lax.broadcasted_iota(jnp.int32, cx.shape, 0)
    before = jnp.where(rid == 0, first, pltpu.roll(cx, 1, 0))
    after = jnp.where(rid == tm - 1, last, pltpu.roll(cx, tm - 1, 0))
    conv = before * cw_ref[0:1, :] + cx * cw_ref[1:2, :] + after * cw_ref[2:3, :]
    y = (bg_ref[...] * conv).astype(BF16)
    _norm_and_route(_dot(y, w_ref[...]), x_ref, g_ref, b_ref, wr_ref, br_ref, h_ref, eid_ref, wt_ref, alpha)


def _outproj_c(bg, cx, cw, x, w, g, b, wr, br, alpha, dims):
    tt, d = x.shape
    tp, sp, ss = dims
    tm = min(512, sp, ss)
    hb = tm // SHORT_HALO
    nh = tt // SHORT_HALO
    row = lambda i: (i, 0)
    out_specs, out_shape = _route_outs(tt, d, tm)
    return pl.pallas_call(
        functools.partial(_outproj_c_kernel, alpha=alpha, dims=dims, tm=tm),
        grid=(tt // tm,),
        in_specs=[pl.BlockSpec((tm, d), row), pl.BlockSpec((tm, d), row),
                  pl.BlockSpec((SHORT_HALO, d), lambda i: (jnp.maximum(i * hb - 1, 0), 0)),
                  pl.BlockSpec((SHORT_HALO, d), lambda i: (jnp.minimum((i + 1) * hb, nh - 1), 0)),
                  _resident(cw.shape), pl.BlockSpec((tm, d), row), _resident(w.shape),
                  _resident((1, d)), _resident((1, d)), _resident(wr.shape), _resident((1, LANES))],
        out_specs=out_specs, out_shape=out_shape,
        compiler_params=_params(),
        name="outproj_c",
    )(bg, cx, cx, cx, cw, x, w, g.reshape(1, d), b.reshape(1, d), wr, br)


def _plan_kernel(eid_ref, pos_ref, texp_ref, base_ref, *, tile, n_tiles_pad):
    phase = pl.program_id(0)
    t = pl.program_id(1)
    tp = eid_ref.shape[0]
    lane = lax.broadcasted_iota(jnp.int32, (tp, LANES), 1)
    e0 = eid_ref[:, 0:1]
    e1 = eid_ref[:, 1:2]
    oh0 = lane == e0
    oh1 = lane == e1
    chosen = jnp.where(oh0 | oh1, 1.0, 0.0)
    counts = jnp.sum(chosen, axis=0, keepdims=True)

    @pl.when((phase == 0) & (t == 0))
    def _():
        base_ref[...] = jnp.zeros_like(base_ref)

    @pl.when(phase == 0)
    def _():
        base_ref[...] += counts

    @pl.when((phase == 1) & (t == 0))
    def _():
        totals = base_ref[...]
        ntile = jnp.floor((totals + (tile - 1)) * (1.0 / tile))
        r = lax.broadcasted_iota(jnp.int32, (LANES, LANES), 0)
        c = lax.broadcasted_iota(jnp.int32, (LANES, LANES), 1)
        upper = jnp.where(r < c, 1.0, 0.0).astype(BF16)
        nt8 = jnp.broadcast_to(ntile, (8, LANES)).astype(BF16)
        start = _dot(nt8, upper)[0:1, :]
        end = start + ntile
        base_ref[...] = start * tile
        j = lax.broadcasted_iota(jnp.int32, (n_tiles_pad, LANES), 0).astype(F32)
        ln = lax.broadcasted_iota(jnp.int32, (n_tiles_pad, LANES), 1)
        done = jnp.where((end <= j) & (ln < N_EXPERTS), 1.0, 0.0)
        texp = jnp.minimum(jnp.sum(done, axis=-1, keepdims=True), N_EXPERTS - 1.0)
        total = jnp.sum(jnp.where(ln[0:1, :] < N_EXPERTS, ntile, 0.0), axis=-1, keepdims=True)
        real_end = jnp.sum(jnp.where(ln.astype(F32) == texp, start * tile + totals, 0.0), axis=-1, keepdims=True)
        nvalid = jnp.clip(real_end - j[:, 0:1] * tile, 0.0, float(tile))
        texp_ref[...] = jnp.where(ln == 1, total, jnp.where(ln == 2, nvalid, texp)).astype(jnp.int32)

    @pl.when(phase == 1)
    def _():
        r = lax.broadcasted_iota(jnp.int32, (tp, tp), 0)
        c = lax.broadcasted_iota(jnp.int32, (tp, tp), 1)
        lower = jnp.where(c < r, 1.0, 0.0).astype(BF16)
        slot = base_ref[...] + _dot(lower, chosen.astype(BF16))
        p0 = jnp.sum(jnp.where(oh0, slot, 0.0), axis=-1, keepdims=True)
        p1 = jnp.sum(jnp.where(oh1, slot, 0.0), axis=-1, keepdims=True)
        pos_ref[...] = jnp.where(lane == 0, p0, jnp.where(lane == 1, p1, 0.0)).astype(jnp.int32)
        base_ref[...] += counts


def _plan(eid, n_tiles_pad):
    tt = eid.shape[0]
    tp = min(512, tt)
    return pl.pallas_call(
        functools.partial(_plan_kernel, tile=MOE_TILE, n_tiles_pad=n_tiles_pad),
        grid=(2, tt // tp),
        in_specs=[pl.BlockSpec((tp, LANES), lambda ph, t: (t, 0))],
        out_specs=[pl.BlockSpec((tp, LANES), lambda ph, t: (t * ph, 0)),
                   pl.BlockSpec((n_tiles_pad, LANES), lambda ph, t: (0, 0))],
        out_shape=[jax.ShapeDtypeStruct((tt, LANES), jnp.int32),
                   jax.ShapeDtypeStruct((n_tiles_pad, LANES), jnp.int32)],
        scratch_shapes=[pltpu.VMEM((1, LANES), F32)],
        compiler_params=_params(2),
        name="moe_plan",
    )(eid)


def _invert_kernel(pos_ref, code_ref, *, tt, n_slots, chunk, tok):
    phase = pl.program_id(0)
    step = pl.program_id(1)

    @pl.when(phase == 0)
    def _():
        def fill(r, _):
            s = jnp.minimum(step * chunk + r, n_slots - 1)
            code_ref[s] = 2 * tt + s
            return 0

        lax.fori_loop(0, chunk, fill, 0)

    @pl.when(phase == 1)
    def _():
        def put(r, _):
            t = step * tok + r
            code_ref[pos_ref[t]] = 2 * t
            code_ref[pos_ref[tt + t]] = 2 * t + 1
            return 0

        lax.fori_loop(0, tok, put, 0)


def _invert(pos_flat, tt, n_slots):
    tok = min(512, tt)
    steps = tt // tok
    chunk = -(-n_slots // steps)
    grid_spec = pltpu.PrefetchScalarGridSpec(
        num_scalar_prefetch=1, grid=(2, steps), in_specs=[],
        out_specs=pl.BlockSpec(memory_space=pltpu.SMEM))
    return pl.pallas_call(
        functools.partial(_invert_kernel, tt=tt, n_slots=n_slots, chunk=chunk, tok=tok),
        grid_spec=grid_spec,
        out_shape=jax.ShapeDtypeStruct((n_slots,), jnp.int32),
        compiler_params=_params(2),
        name="moe_invert",
    )(pos_flat)


def _moe_kernel(texp_ref, nused_ref, nvalid_ref, code_ref, h_ref, wup_ref, wdn_ref, y_ref,
                xbuf, obuf, gsem, ssem, *, tt, tm):
    j = pl.program_id(0)
    nused = nused_ref[0]
    slot = j % 2

    def gather_copy(tile, buf, r):
        code = code_ref[tile * tm + r]
        src = jnp.where(code >= 2 * tt, 0, code // 2)
        return pltpu.make_async_copy(h_ref.at[pl.ds(src, 1)], xbuf.at[buf, pl.ds(r, 1)], gsem.at[buf])

    def scatter_copy(tile, buf, r):
        dst = code_ref[tile * tm + r]
        return pltpu.make_async_copy(obuf.at[buf, pl.ds(r, 1)], y_ref.at[pl.ds(dst, 1)], ssem.at[buf])

    def for_rows(n, fn):
        def body(r, _):
            fn(r)
            return 0
        lax.fori_loop(0, n, body, 0)

    def real_rows(tile):
        return nvalid_ref[tile]

    @pl.when(j == 0)
    def _():
        for_rows(tm, lambda r: gather_copy(0, 0, r).start())

    @pl.when(j + 1 < nused)
    def _():
        for_rows(tm, lambda r: gather_copy(j + 1, 1 - slot, r).start())

    @pl.when(j < nused)
    def _():
        for_rows(tm, lambda r: gather_copy(j, slot, r).wait())
        x = xbuf[slot].astype(BF16)
        up = _dot(x, wup_ref[0])
        a = up[:, :D_EXPERT]
        hid = (a * jax.nn.sigmoid(a) * up[:, D_EXPERT:]).astype(BF16)
        out = _dot(hid, wdn_ref[0])

        @pl.when(j >= 2)
        def _():
            for_rows(real_rows(j - 2), lambda r: scatter_copy(j - 2, slot, r).wait())

        obuf[slot] = out
        for_rows(real_rows(j), lambda r: scatter_copy(j, slot, r).start())

    @pl.when(j == nused - 1)
    def _():
        @pl.when(j >= 1)
        def _():
            for_rows(real_rows(j - 1), lambda r: scatter_copy(j - 1, 1 - slot, r).wait())

        for_rows(real_rows(j), lambda r: scatter_copy(j, slot, r).wait())


def _moe(texp, nused, nvalid, code, h, wup, wdn, n_tiles):
    tt, d = h.shape
    tm = MOE_TILE

    def wmap(j, texp, nused, nvalid, code):
        return (texp[jnp.minimum(j, nused[0] - 1)], 0, 0)

    grid_spec = pltpu.PrefetchScalarGridSpec(
        num_scalar_prefetch=4,
        grid=(n_tiles,),
        in_specs=[pl.BlockSpec(memory_space=pl.ANY),
                  pl.BlockSpec((1,) + wup.shape[1:], wmap),
                  pl.BlockSpec((1,) + wdn.shape[1:], wmap)],
        out_specs=pl.BlockSpec(memory_space=pl.ANY),
        scratch_shapes=[pltpu.VMEM((2, tm, d), F32), pltpu.VMEM((2, tm, d), F32),
                        pltpu.SemaphoreType.DMA((2,)), pltpu.SemaphoreType.DMA((2,))],
    )
    return pl.pallas_call(
        functools.partial(_moe_kernel, tt=tt, tm=tm),
        grid_spec=grid_spec,
        out_shape=jax.ShapeDtypeStruct((2 * tt, d), F32),
        compiler_params=_params(),
        name="moe_experts",
    )(texp, nused, nvalid, code, h, wup, wdn)


def _combine_kernel(h_ref, ya_ref, yb_ref, wt_ref, p_ref, g_ref, b_ref, wg_ref, bg_ref, wp_ref, o_ref, *, alpha):
    wt = wt_ref[...]
    moe = ya_ref[...] * wt[:, 0:1] + yb_ref[...] * wt[:, 1:2]
    h2 = _layer_norm(alpha * h_ref[...] + moe, g_ref[...], b_ref[...])
    gate = jax.nn.sigmoid(_dot(h2.astype(BF16), wg_ref[...]) + bg_ref[...])
    o_ref[...] = h2 + gate * _dot(p_ref[...].astype(BF16), wp_ref[...])


def _combine(h, y2, wt, p, g, b, wg, bg, wp, alpha):
    tt, d = h.shape
    tm = min(512, tt)
    row = lambda i: (i, 0)
    y2 = y2.reshape(y2.shape[0] // 2, 2 * d)
    return pl.pallas_call(
        functools.partial(_combine_kernel, alpha=alpha),
        grid=(tt // tm,),
        in_specs=[pl.BlockSpec((tm, d), row), pl.BlockSpec((tm, d), lambda i: (i, 0)),
                  pl.BlockSpec((tm, d), lambda i: (i, 1)), pl.BlockSpec((tm, LANES), row),
                  pl.BlockSpec((tm, p.shape[1]), row), _resident((1, d)), _resident((1, d)),
                  _resident(wg.shape), _resident((1, d)), _resident(wp.shape)],
        out_specs=pl.BlockSpec((tm, d), row),
        out_shape=jax.ShapeDtypeStruct((tt, d), F32),
        compiler_params=_params(),
        name="moe_combine_ple",
    )(h, y2, y2, wt, p, g.reshape(1, d), b.reshape(1, d), wg, bg.reshape(1, d), wp)


def _moe_block(h, eid, wt, wup, wdn):
    tt = h.shape[0]
    n_tiles = (2 * tt + N_EXPERTS * (MOE_TILE - 1)) // MOE_TILE + 1
    n_tiles_pad = -(-n_tiles // 8) * 8
    pos, tmeta = _plan(eid, n_tiles_pad)
    pos_flat = jnp.concatenate([pos[:, 0], pos[:, 1]])
    code = _invert(pos_flat, tt, n_tiles * MOE_TILE)
    return _moe(tmeta[:, 0], tmeta[0:1, 1], tmeta[:, 2], code, h, wup, wdn, n_tiles)


def kernel(x_prompt, x_sample, p_prompt, p_sample, w_in_a, sink, conv_dw_a, conv_b_a, conv_ln_g, conv_ln_b,
           w_out_a, w_in_c, conv_w_c, w_out_c, ln1_g, ln1_b, ln2_g, ln2_b, w_router_group, b_router_group,
           w_router_expert, b_router_expert, w_up, w_down, w_ple, w_ple_gate, b_ple_gate):
    bp, sp, d = x_prompt.shape
    bs, ss, _ = x_sample.shape
    depth = p_prompt.shape[0]
    tp = bp * sp
    dims = (tp, sp, ss)
    alpha = (2.0 * depth) ** 0.25

    x = jnp.concatenate([x_prompt.reshape(tp, d), x_sample.reshape(bs * ss, d)], axis=0)
    p = jnp.concatenate([p_prompt.reshape(depth, tp, -1), p_sample.reshape(depth, bs * ss, -1)], axis=1)
    tabs = _rope_tables(max(sp, ss))
    pad = LANES - N_GROUPS - N_EXPERTS
    w_route = jnp.concatenate(
        [w_router_group, w_router_expert, jnp.zeros((depth, d, pad), F32)], axis=-1).astype(BF16)
    b_route = jnp.concatenate(
        [b_router_group, b_router_expert, jnp.zeros((depth, pad), F32)], axis=-1).reshape(depth, 1, LANES)

    for i in range(depth):
        j = i // 2
        wr, br = w_route[i], b_route[i]
        if i % 2 == 0:
            q, k, v, z = _inproj_a(x, w_in_a[j].astype(BF16), tabs, dims)
            att = _attention(q, k, v, sink[j], dims)
            cv = _conv_a(z, conv_dw_a[j], conv_b_a[j], conv_ln_g[j], conv_ln_b[j], dims)
            h, eid, wt = _outproj_a(att, cv, x, w_out_a[j].astype(BF16), ln1_g[i], ln1_b[i], wr, br, alpha)
        else:
            bg, cx = _inproj_c(x, w_in_c[j].astype(BF16))
            h, eid, wt = _outproj_c(bg, cx, conv_w_c[j], x, w_out_c[j].astype(BF16), ln1_g[i], ln1_b[i],
                                    wr, br, alpha, dims)
        y2 = _moe_block(h, eid, wt, w_up[i].astype(BF16), w_down[i].astype(BF16))
        x = _combine(h, y2, wt, p[i], ln2_g[i], ln2_b[i], w_ple_gate[i].astype(BF16), b_ple_gate[i],
                     w_ple[i].astype(BF16), alpha)

    return x[:tp].reshape(bp, sp, d), x[tp:].reshape(bs, ss, d)
```

```python
import functools

import jax
import jax.numpy as jnp
from jax import lax
from jax.experimental import pallas as pl
from jax.experimental.pallas import tpu as pltpu

F32 = jnp.float32
BF16 = jnp.bfloat16

N_HEADS = 16
N_KV_HEADS = 4
HEAD_DIM = 64
Q_WIDTH = N_HEADS * HEAD_DIM
KV_WIDTH = N_KV_HEADS * HEAD_DIM
BLOCK = 128
ROT_DIM = HEAD_DIM // 4
ROPE_THETA = 500000.0
CONV_WIDTH = 1024
CONV_KSIZE = 31
CONV_HALO = 16
SHORT_HALO = 8
N_GROUPS = 4
EXPERTS_PER_GROUP = 8
N_EXPERTS = N_GROUPS * EXPERTS_PER_GROUP
D_EXPERT = 512
LN_EPS = 1e-5
NEG_INF = -1e30
LANES = 128
SUBLANES = 8

MOE_TILE = 256
DMA_UNROLL = 8
VMEM_LIMIT = 56 * 1024 * 1024


def _params(n_axes=1):
    return pltpu.CompilerParams(dimension_semantics=("arbitrary",) * n_axes,
                                vmem_limit_bytes=VMEM_LIMIT)


def _resident(shape):
    nd = len(shape)
    return pl.BlockSpec(shape, lambda *_: (0,) * nd, pipeline_mode=pl.Buffered(1))


def _dot(a, b):
    return jnp.dot(a, b, preferred_element_type=F32)


def _layer_norm(y, g, b):
    mu = jnp.mean(y, axis=-1, keepdims=True)
    d = y - mu
    var = jnp.mean(d * d, axis=-1, keepdims=True)
    return d * lax.rsqrt(var + LN_EPS) * g + b


def _seq_offset(r0, tp, sp, ss):
    in_p = r0 < tp
    off = jnp.where(in_p, r0 % sp, (r0 - tp) % ss)
    return off, jnp.where(in_p, sp, ss)


def _row_specs(parts, tm, width):
    if len(parts) == 1:
        return [pl.BlockSpec((tm, width), lambda i, *_: (i, 0))]
    nb1 = parts[0].shape[0] // tm
    return [pl.BlockSpec((tm, width), lambda i, *_: (jnp.minimum(i, nb1 - 1), 0)),
            pl.BlockSpec((tm, width), lambda i, *_: (jnp.maximum(i - nb1, 0), 0))]


def _read_rows(refs, nb1):
    if len(refs) == 1:
        return refs[0][...]
    return jnp.where(pl.program_id(0) < nb1, refs[0][...], refs[1][...])


def _nb1(parts, tm):
    return parts[0].shape[0] // tm


def _pack_rows(x):
    n = x.shape[1] // 2
    lo = lax.bitcast_convert_type(x[:, :n].astype(BF16).astype(F32), jnp.uint32) >> 16
    hi = lax.bitcast_convert_type(x[:, n:].astype(BF16).astype(F32), jnp.uint32) & jnp.uint32(0xFFFF0000)
    return lo | hi


def _unpack_rows(w):
    lo = lax.bitcast_convert_type(w << 16, F32)
    hi = lax.bitcast_convert_type(w & jnp.uint32(0xFFFF0000), F32)
    return jnp.concatenate([lo, hi], axis=1)


def _store_token_tiles(ref, w):
    m = w.shape[0]
    for s in range(SUBLANES):
        ref[pl.ds(s, m, stride=SUBLANES), :] = w[:, LANES * s:LANES * (s + 1)]


def _load_token_tiles(ref, m):
    return jnp.concatenate([ref[pl.ds(s, m, stride=SUBLANES), :] for s in range(SUBLANES)], axis=1)


def _inproj_a_kernel(*refs, n_x, nb1):
    x_refs, (w_ref, rc_ref, ra_ref, rb_ref, q_ref, k_ref, v_ref, z_ref) = refs[:n_x], refs[n_x:]
    xb = _read_rows(x_refs, nb1).astype(BF16)
    rc, ra, rb = rc_ref[...], ra_ref[...], rb_ref[...]

    def rope(u):
        outs = []
        half = ROT_DIM // 2
        for j in range(u.shape[1] // LANES):
            t = u[:, LANES * j:LANES * (j + 1)]
            outs.append(t * rc + pltpu.roll(t, LANES - half, 1) * ra + pltpu.roll(t, half, 1) * rb)
        return jnp.concatenate(outs, axis=1)

    cw = 512
    for c in range(Q_WIDTH // cw):
        u = _dot(xb, w_ref[:, cw * c:cw * (c + 1)])
        q_ref[:, cw * c:cw * (c + 1)] = (rope(u) * (HEAD_DIM ** -0.5)).astype(BF16)
    u = _dot(xb, w_ref[:, Q_WIDTH:Q_WIDTH + 2 * KV_WIDTH])
    k_ref[...] = rope(u[:, :KV_WIDTH]).astype(BF16)
    v_ref[...] = u[:, KV_WIDTH:].astype(BF16)
    base = Q_WIDTH + 2 * KV_WIDTH
    for c in range(CONV_WIDTH // cw):
        a = _dot(xb, w_ref[:, base + cw * c:base + cw * (c + 1)])
        g = _dot(xb, w_ref[:, base + CONV_WIDTH + cw * c:base + CONV_WIDTH + cw * (c + 1)])
        z_ref[:, cw * c:cw * (c + 1)] = a * jax.nn.sigmoid(g)


def _rope_tables(smax):
    half = ROT_DIM // 2
    inv_freq = ROPE_THETA ** (-(jnp.arange(half, dtype=F32) * 2.0 / ROT_DIM))
    ang = jnp.arange(smax, dtype=F32)[:, None] * inv_freq[None, :]
    cos, sin = jnp.cos(ang), jnp.sin(ang)
    d = jnp.arange(LANES) % HEAD_DIM
    f = d % half
    rc = jnp.where(d[None, :] < ROT_DIM, cos[:, f], 1.0)
    ra = jnp.where(d[None, :] < half, -sin[:, f], 0.0)
    rb = jnp.where((d[None, :] >= half) & (d[None, :] < ROT_DIM), sin[:, f], 0.0)
    return rc.astype(F32), ra.astype(F32), rb.astype(F32)


def _inproj_a(x_parts, w, tabs, dims):
    tp, sp, ss = dims
    tt = sum(a.shape[0] for a in x_parts)
    d = x_parts[0].shape[1]
    tm = min(512, sp, ss)
    n_in = w.shape[1]

    def tab_map(i):
        off, _ = _seq_offset(i * tm, tp, sp, ss)
        return (off // tm, 0)

    row = lambda i: (i, 0)
    tab_spec = pl.BlockSpec((tm, LANES), tab_map)
    return pl.pallas_call(
        functools.partial(_inproj_a_kernel, n_x=len(x_parts), nb1=_nb1(x_parts, tm)),
        grid=(tt // tm,),
        in_specs=_row_specs(x_parts, tm, d) + [_resident((d, n_in)), tab_spec, tab_spec, tab_spec],
        out_specs=[pl.BlockSpec((tm, Q_WIDTH), row), pl.BlockSpec((tm, KV_WIDTH), row),
                   pl.BlockSpec((tm, KV_WIDTH), row), pl.BlockSpec((tm, CONV_WIDTH), row)],
        out_shape=[jax.ShapeDtypeStruct((tt, Q_WIDTH), BF16), jax.ShapeDtypeStruct((tt, KV_WIDTH), BF16),
                   jax.ShapeDtypeStruct((tt, KV_WIDTH), BF16), jax.ShapeDtypeStruct((tt, CONV_WIDTH), F32)],
        compiler_params=_params(),
        name="inproj_a",
    )(*x_parts, w, *tabs)


def _attn_kernel(sink_ref, q_ref, kp_ref, kc_ref, kn_ref, vp_ref, vc_ref, vn_ref, o_ref, s_scr, p_scr, *, dims):
    tp, sp, ss = dims
    off, slen = _seq_offset(pl.program_id(0) * BLOCK, tp, sp, ss)
    no_prev = jnp.where(off > 0, 0, BLOCK)
    no_next = jnp.where(off + BLOCK < slen, 0, BLOCK)
    row = lax.broadcasted_iota(jnp.int32, (BLOCK, 3 * BLOCK), 0)
    col = lax.broadcasted_iota(jnp.int32, (BLOCK, 3 * BLOCK), 1)
    jj = col % BLOCK
    blk = col // BLOCK
    valid = ((blk == 1) | ((blk == 0) & (jj >= row + no_prev)) | ((blk == 2) & (jj + no_next <= row)))
    kcat = jnp.concatenate([kp_ref[...], kc_ref[...], kn_ref[...]], axis=0)
    vcat = jnp.concatenate([vp_ref[...], vc_ref[...], vn_ref[...]], axis=0)
    grp = N_HEADS // N_KV_HEADS

    for hh in range(N_HEADS):
        h = hh // grp
        s_scr[hh] = lax.dot_general(q_ref[:, HEAD_DIM * hh:HEAD_DIM * (hh + 1)],
                                    kcat[:, HEAD_DIM * h:HEAD_DIM * (h + 1)],
                                    (((1,), (1,)), ((), ())), preferred_element_type=F32)
    sk = sink_ref[:, 0:1, 0:1]
    s = jnp.where(valid[None], s_scr[...], NEG_INF)
    m = jnp.maximum(jnp.max(s, axis=-1, keepdims=True), sk)
    p = jnp.exp(s - m)
    denom = jnp.sum(p, axis=-1, keepdims=True) + jnp.exp(sk - m)
    p_scr[...] = p.astype(BF16)
    per_vreg = LANES // HEAD_DIM
    for c in range(N_HEADS // per_vreg):
        outs = []
        for hh in range(per_vreg * c, per_vreg * (c + 1)):
            h = hh // grp
            outs.append(_dot(p_scr[hh], vcat[:, HEAD_DIM * h:HEAD_DIM * (h + 1)]) / denom[hh])
        o_ref[:, LANES * c:LANES * (c + 1)] = jnp.concatenate(outs, axis=1).astype(BF16)


def _attention(q, k, v, sink, dims):
    tt = q.shape[0]
    nb = tt // BLOCK
    cur = lambda i: (i, 0)
    prv = lambda i: (jnp.maximum(i - 1, 0), 0)
    nxt = lambda i: (jnp.minimum(i + 1, nb - 1), 0)
    kv = lambda m: pl.BlockSpec((BLOCK, KV_WIDTH), m)
    sink_b = jnp.broadcast_to(sink.astype(F32)[:, None, None], (N_HEADS, SUBLANES, LANES))
    return pl.pallas_call(
        functools.partial(_attn_kernel, dims=dims),
        grid=(nb,),
        in_specs=[_resident(sink_b.shape), pl.BlockSpec((BLOCK, Q_WIDTH), cur),
                  kv(prv), kv(cur), kv(nxt), kv(prv), kv(cur), kv(nxt)],
        out_specs=pl.BlockSpec((BLOCK, Q_WIDTH), cur),
        out_shape=jax.ShapeDtypeStruct((tt, Q_WIDTH), BF16),
        scratch_shapes=[pltpu.VMEM((N_HEADS, BLOCK, 3 * BLOCK), F32),
                        pltpu.VMEM((N_HEADS, BLOCK, 3 * BLOCK), BF16)],
        compiler_params=_params(),
        name="band_attention",
    )(sink_b, q, k, k, k, v, v, v)


def _conv_a_kernel(z_ref, zp_ref, zn_ref, w_ref, b_ref, g_ref, beta_ref, o_ref, ext_ref, *, dims, ts):
    tp, sp, ss = dims
    off, slen = _seq_offset(pl.program_id(0) * ts, tp, sp, ss)
    keep_prev = (off > 0).astype(F32)
    keep_next = (off + ts < slen).astype(F32)
    ext_ref[0:CONV_HALO, :] = zp_ref[...] * keep_prev
    ext_ref[CONV_HALO:CONV_HALO + ts, :] = z_ref[...]
    ext_ref[CONV_HALO + ts:, :] = zn_ref[...] * keep_next
    rows, lw = 32, 256
    n_a = -(-CONV_KSIZE // SUBLANES)
    for rc in range(ts // rows):
        r0 = rows * rc
        accs = []
        for lc in range(CONV_WIDTH // lw):
            ls = slice(lw * lc, lw * (lc + 1))
            acc = jnp.zeros((rows, lw), F32)
            for b in range(SUBLANES):
                part = jnp.zeros((rows + SUBLANES, lw), F32)
                for a in range(n_a):
                    k = SUBLANES * a + b
                    if k < CONV_KSIZE:
                        lo = r0 + SUBLANES * a
                        part = part + ext_ref[lo:lo + rows + SUBLANES, ls] * w_ref[k:k + 1, ls]
                acc = acc + part[b + 1:b + 1 + rows, :]
            accs.append(acc + b_ref[:, ls])
        y = _layer_norm(jnp.concatenate(accs, axis=1), g_ref[...], beta_ref[...])
        o_ref[r0:r0 + rows, :] = (y * jax.nn.sigmoid(y)).astype(BF16)


def _conv_a(z, w, b, g, beta, dims):
    tt, cw = z.shape
    tp, sp, ss = dims
    ts = min(256, sp, ss)
    hb = ts // CONV_HALO
    nh = tt // CONV_HALO
    vec = lambda a: a.reshape(1, cw)
    return pl.pallas_call(
        functools.partial(_conv_a_kernel, dims=dims, ts=ts),
        grid=(tt // ts,),
        in_specs=[pl.BlockSpec((ts, cw), lambda i: (i, 0)),
                  pl.BlockSpec((CONV_HALO, cw), lambda i: (jnp.maximum(i * hb - 1, 0), 0)),
                  pl.BlockSpec((CONV_HALO, cw), lambda i: (jnp.minimum((i + 1) * hb, nh - 1), 0)),
                  _resident((CONV_KSIZE, cw)), _resident((1, cw)), _resident((1, cw)), _resident((1, cw))],
        out_specs=pl.BlockSpec((ts, cw), lambda i: (i, 0)),
        out_shape=jax.ShapeDtypeStruct((tt, cw), BF16),
        scratch_shapes=[pltpu.VMEM((ts + 2 * CONV_HALO, cw), F32)],
        compiler_params=_params(),
        name="conformer_conv",
    )(z, z, z, w, vec(b), vec(g), vec(beta))


def _norm_and_route(mix, x, g_ref, b_ref, wr_ref, br_ref, outs, alpha):
    h_ref, hp_ref, eid_ref, wt_ref = outs
    h = _layer_norm(alpha * x + mix, g_ref[...], b_ref[...])
    h_ref[...] = h
    _store_token_tiles(hp_ref, _pack_rows(h))
    logits = _dot(h.astype(BF16), wr_ref[...]) + br_ref[...]
    lane = lax.broadcasted_iota(jnp.int32, logits.shape, 1).astype(F32)
    big = jnp.float32(LANES)
    ninf = jnp.float32(-jnp.inf)
    gmask = lane < N_GROUPS
    lg = jnp.where(gmask, logits, ninf)
    gmax = jnp.max(lg, axis=-1, keepdims=True)
    gidx = jnp.min(jnp.where(lg == gmax, lane, big), axis=-1, keepdims=True)
    gsum = jnp.sum(jnp.where(gmask, jnp.exp(logits - gmax), 0.0), axis=-1, keepdims=True)
    lo = N_GROUPS + EXPERTS_PER_GROUP * gidx
    le = jnp.where((lane >= lo) & (lane < lo + EXPERTS_PER_GROUP), logits, ninf)
    t0 = jnp.max(le, axis=-1, keepdims=True)
    i0 = jnp.min(jnp.where(le == t0, lane, big), axis=-1, keepdims=True)
    le1 = jnp.where(lane == i0, ninf, le)
    t1 = jnp.max(le1, axis=-1, keepdims=True)
    i1 = jnp.min(jnp.where(le1 == t1, lane, big), axis=-1, keepdims=True)
    r = jnp.exp(t1 - t0)
    w0 = 1.0 / ((1.0 + r) * gsum)
    w1 = r * w0
    eid_ref[...] = jnp.where(lane == 0, i0 - N_GROUPS, jnp.where(lane == 1, i1 - N_GROUPS, 0.0)).astype(jnp.int32)
    wt_ref[...] = jnp.where(lane == 0, w0, jnp.where(lane == 1, w1, 0.0))


def _outproj_a_kernel(*refs, alpha, n_x, nb1):
    att_ref, cv_ref = refs[:2]
    x_refs = refs[2:2 + n_x]
    w_ref, g_ref, b_ref, wr_ref, br_ref = refs[2 + n_x:7 + n_x]
    mix = _dot(att_ref[...], w_ref[0:Q_WIDTH, :]) + _dot(cv_ref[...], w_ref[Q_WIDTH:, :])
    _norm_and_route(mix, _read_rows(x_refs, nb1), g_ref, b_ref, wr_ref, br_ref, refs[7 + n_x:], alpha)


def _route_outs(tt, d, tm):
    row = lambda i: (i, 0)
    specs = [pl.BlockSpec((tm, d), row), pl.BlockSpec((tm * SUBLANES, LANES), row),
             pl.BlockSpec((tm, LANES), row), pl.BlockSpec((tm, LANES), row)]
    shapes = [jax.ShapeDtypeStruct((tt, d), F32), jax.ShapeDtypeStruct((tt * SUBLANES, LANES), jnp.uint32),
              jax.ShapeDtypeStruct((tt, LANES), jnp.int32), jax.ShapeDtypeStruct((tt, LANES), F32)]
    return specs, shapes


def _outproj_a(att, cv, x_parts, w, g, b, wr, br, alpha):
    tt = att.shape[0]
    d = w.shape[1]
    tm = min(512, min(a.shape[0] for a in x_parts))
    row = lambda i: (i, 0)
    out_specs, out_shape = _route_outs(tt, d, tm)
    return pl.pallas_call(
        functools.partial(_outproj_a_kernel, alpha=alpha, n_x=len(x_parts), nb1=_nb1(x_parts, tm)),
        grid=(tt // tm,),
        in_specs=[pl.BlockSpec((tm, Q_WIDTH), row), pl.BlockSpec((tm, CONV_WIDTH), row)]
        + _row_specs(x_parts, tm, d)
        + [_resident(w.shape), _resident((1, d)), _resident((1, d)), _resident(wr.shape), _resident((1, LANES))],
        out_specs=out_specs, out_shape=out_shape,
        compiler_params=_params(),
        name="outproj_a",
    )(att, cv, *x_parts, w, g.reshape(1, d), b.reshape(1, d), wr, br)


def _inproj_c_kernel(x_ref, w_ref, bg_ref, cx_ref):
    xb = x_ref[...].astype(BF16)
    d = x_ref.shape[1]
    cw = 512
    for c in range(d // cw):
        cs = slice(cw * c, cw * (c + 1))
        bg_ref[:, cs] = _dot(xb, w_ref[:, cw * c:cw * (c + 1)])
        cg = _dot(xb, w_ref[:, d + cw * c:d + cw * (c + 1)])
        xv = _dot(xb, w_ref[:, 2 * d + cw * c:2 * d + cw * (c + 1)])
        cx_ref[:, cs] = cg * xv


def _inproj_c(x, w):
    tt, d = x.shape
    tm = min(256, tt)
    row = lambda i: (i, 0)
    return pl.pallas_call(
        _inproj_c_kernel,
        grid=(tt // tm,),
        in_specs=[pl.BlockSpec((tm, d), row), _resident(w.shape)],
        out_specs=[pl.BlockSpec((tm, d), row), pl.BlockSpec((tm, d), row)],
        out_shape=[jax.ShapeDtypeStruct((tt, d), F32), jax.ShapeDtypeStruct((tt, d), F32)],
        compiler_params=_params(),
        name="inproj_c",
    )(x, w)


def _outproj_c_kernel(bg_ref, cx_ref, cp_ref, cn_ref, cw_ref, x_ref, w_ref, g_ref, b_ref, wr_ref, br_ref,
                      *outs, alpha, dims, tm):
    tp, sp, ss = dims
    off, slen = _seq_offset(pl.program_id(0) * tm, tp, sp, ss)
    keep_prev = (off > 0).astype(F32)
    keep_next = (off + tm < slen).astype(F32)
    cx = cx_ref[...]
    first = cp_ref[SHORT_HALO - 1:SHORT_HALO, :] * keep_prev
    last = cn_ref[0:1, :] * keep_next
    rid = lax.broadcasted_iota(jnp.int32, cx.shape, 0)
    before = jnp.where(rid == 0, first, pltpu.roll(cx, 1, 0))
    after = jnp.where(rid == tm - 1, last, pltpu.roll(cx, tm - 1, 0))
    conv = before * cw_ref[0:1, :] + cx * cw_ref[1:2, :] + after * cw_ref[2:3, :]
    y = (bg_ref[...] * conv).astype(BF16)
    _norm_and_route(_dot(y, w_ref[...]), x_ref[...], g_ref, b_ref, wr_ref, br_ref, outs, alpha)


def _outproj_c(bg, cx, cw, x, w, g, b, wr, br, alpha, dims):
    tt, d = x.shape
    tp, sp, ss = dims
    tm = min(512, sp, ss)
    hb = tm // SHORT_HALO
    nh = tt // SHORT_HALO
    row = lambda i: (i, 0)
    out_specs, out_shape = _route_outs(tt, d, tm)
    return pl.pallas_call(
        functools.partial(_outproj_c_kernel, alpha=alpha, dims=dims, tm=tm),
        grid=(tt // tm,),
        in_specs=[pl.BlockSpec((tm, d), row), pl.BlockSpec((tm, d), row),
                  pl.BlockSpec((SHORT_HALO, d), lambda i: (jnp.maximum(i * hb - 1, 0), 0)),
                  pl.BlockSpec((SHORT_HALO, d), lambda i: (jnp.minimum((i + 1) * hb, nh - 1), 0)),
                  _resident(cw.shape), pl.BlockSpec((tm, d), row), _resident(w.shape),
                  _resident((1, d)), _resident((1, d)), _resident(wr.shape), _resident((1, LANES))],
        out_specs=out_specs, out_shape=out_shape,
        compiler_params=_params(),
        name="outproj_c",
    )(bg, cx, cx, cx, cw, x, w, g.reshape(1, d), b.reshape(1, d), wr, br)


def _plan_kernel(eid_ref, pos_ref, tmeta_ref, pbase_ref, cbase_ref, *, tile, n_tiles_pad, n_codes):
    phase = pl.program_id(0)
    t = pl.program_id(1)
    tp = eid_ref.shape[0]
    lane = lax.broadcasted_iota(jnp.int32, (tp, LANES), 1)
    oh0 = lane == eid_ref[:, 0:1]
    oh1 = lane == eid_ref[:, 1:2]
    chosen = jnp.where(oh0 | oh1, 1.0, 0.0)
    counts = jnp.sum(chosen, axis=0, keepdims=True)

    @pl.when((phase == 0) & (t == 0))
    def _():
        pbase_ref[...] = jnp.zeros_like(pbase_ref)

    @pl.when(phase == 0)
    def _():
        pbase_ref[...] += counts

    @pl.when((phase == 1) & (t == 0))
    def _():
        totals = pbase_ref[...]
        ntile = jnp.floor((totals + (tile - 1)) * (1.0 / tile))
        hi = jnp.floor(totals * (1.0 / 256.0))
        lo = totals - 256.0 * hi
        r = lax.broadcasted_iota(jnp.int32, (LANES, LANES), 0)
        c = lax.broadcasted_iota(jnp.int32, (LANES, LANES), 1)
        upper = jnp.where(r < c, 1.0, 0.0).astype(BF16)

        def prefix(v):
            return _dot(jnp.broadcast_to(v, (SUBLANES, LANES)).astype(BF16), upper)[0:1, :]

        tstart = prefix(ntile)
        cstart = 256.0 * prefix(hi) + prefix(lo)
        pbase_ref[...] = tstart * tile
        cbase_ref[...] = cstart
        j = lax.broadcasted_iota(jnp.int32, (n_tiles_pad, LANES), 0).astype(F32)
        ln = lax.broadcasted_iota(jnp.int32, (n_tiles_pad, LANES), 1)
        done = jnp.where((tstart + ntile <= j) & (ln < N_EXPERTS), 1.0, 0.0)
        texp = jnp.minimum(jnp.sum(done, axis=-1, keepdims=True), N_EXPERTS - 1.0)
        total = jnp.sum(jnp.where(ln[0:1, :] < N_EXPERTS, ntile, 0.0), axis=-1, keepdims=True)
        mine = ln.astype(F32) == texp
        pick = lambda v: jnp.sum(jnp.where(mine, v, 0.0), axis=-1, keepdims=True)
        first = pick(cstart) + (j[:, 0:1] - pick(tstart)) * tile
        first = jnp.clip(first, 0.0, n_codes - 1.0)
        tmeta_ref[...] = jnp.where(ln == 1, total, jnp.where(ln == 2, first, texp)).astype(jnp.int32)

    @pl.when(phase == 1)
    def _():
        r = lax.broadcasted_iota(jnp.int32, (tp, tp), 0)
        c = lax.broadcasted_iota(jnp.int32, (tp, tp), 1)
        lower = jnp.where(c < r, 1.0, 0.0).astype(BF16)
        rank = _dot(lower, chosen.astype(BF16))
        pslot = pbase_ref[...] + rank
        cslot = cbase_ref[...] + rank
        pick = lambda oh, v: jnp.sum(jnp.where(oh, v, 0.0), axis=-1, keepdims=True)
        out = jnp.where(lane == 0, pick(oh0, pslot), jnp.where(lane == 1, pick(oh1, pslot),
              jnp.where(lane == 2, pick(oh0, cslot), jnp.where(lane == 3, pick(oh1, cslot), 0.0))))
        pos_ref[...] = out.astype(jnp.int32)
        pbase_ref[...] += counts
        cbase_ref[...] += counts


def _plan(eid, n_tiles_pad):
    tt = eid.shape[0]
    tp = min(512, tt)
    return pl.pallas_call(
        functools.partial(_plan_kernel, tile=MOE_TILE, n_tiles_pad=n_tiles_pad, n_codes=2 * tt),
        grid=(2, tt // tp),
        in_specs=[pl.BlockSpec((tp, LANES), lambda ph, t: (t, 0))],
        out_specs=[pl.BlockSpec((tp, LANES), lambda ph, t: (t * ph, 0)),
                   pl.BlockSpec((n_tiles_pad, LANES), lambda ph, t: (0, 0))],
        out_shape=[jax.ShapeDtypeStruct((tt, LANES), jnp.int32),
                   jax.ShapeDtypeStruct((n_tiles_pad, LANES), jnp.int32)],
        scratch_shapes=[pltpu.VMEM((1, LANES), F32), pltpu.VMEM((1, LANES), F32)],
        compiler_params=_params(2),
        name="moe_plan",
    )(eid)


def _invert_kernel(cpos_ref, code_ref, *, tt, tok):
    step = pl.program_id(0)

    @pl.when(step == 0)
    def _():
        for r in range(MOE_TILE):
            code_ref[2 * tt + r] = 0

    def put(r, _):
        t = step * tok + r
        code_ref[cpos_ref[t]] = SUBLANES * t
        code_ref[cpos_ref[tt + t]] = SUBLANES * t
        return 0

    lax.fori_loop(0, tok, put, 0, unroll=DMA_UNROLL)


def _invert(cpos_flat, tt):
    tok = min(512, tt)
    grid_spec = pltpu.PrefetchScalarGridSpec(
        num_scalar_prefetch=1, grid=(tt // tok,), in_specs=[],
        out_specs=pl.BlockSpec(memory_space=pltpu.SMEM))
    return pl.pallas_call(
        functools.partial(_invert_kernel, tt=tt, tok=tok),
        grid_spec=grid_spec,
        out_shape=jax.ShapeDtypeStruct((2 * tt + MOE_TILE,), jnp.int32),
        compiler_params=_params(),
        name="moe_invert",
    )(cpos_flat)


def _moe_kernel(texp_ref, nused_ref, first_ref, code_ref, hp_ref, wup_ref, wdn_ref, o_ref, xbuf, gsem,
                *, tm):
    j = pl.program_id(0)
    nused = nused_ref[0]
    slot = j % 2

    def gather(tile, buf, rows):
        first = first_ref[tile]
        for r in rows:
            src = pl.multiple_of(code_ref[first + r], SUBLANES)
            pltpu.make_async_copy(hp_ref.at[pl.ds(src, SUBLANES)],
                                  xbuf.at[buf, pl.ds(SUBLANES * r, SUBLANES)], gsem.at[buf]).start()

    def wait_tile(buf):
        pltpu.make_async_copy(hp_ref.at[pl.ds(0, SUBLANES * tm)], xbuf.at[buf], gsem.at[buf]).wait()

    @pl.when(j == 0)
    def _():
        gather(0, 0, range(tm))

    @pl.when(j < nused)
    def _():
        gather(j + 1, 1 - slot, range(tm))
        wait_tile(slot)
        xb = _unpack_rows(_load_token_tiles(xbuf.at[slot], tm)).astype(BF16)
        up = _dot(xb, wup_ref[0])
        a = up[:, :D_EXPERT]
        hid = (a * jax.nn.sigmoid(a) * up[:, D_EXPERT:]).astype(BF16)
        _store_token_tiles(o_ref, _pack_rows(_dot(hid, wdn_ref[0])))

    @pl.when(j == nused)
    def _():
        wait_tile(slot)

    @pl.when(j >= nused)
    def _():
        o_ref[...] = jnp.zeros_like(o_ref)


def _moe(texp, nused, first, code, hp, wup, wdn, n_tiles):
    tm = MOE_TILE

    def wmap(j, texp, nused, first, code):
        return (texp[jnp.minimum(j, nused[0] - 1)], 0, 0)

    grid_spec = pltpu.PrefetchScalarGridSpec(
        num_scalar_prefetch=4,
        grid=(n_tiles + 1,),
        in_specs=[pl.BlockSpec(memory_space=pl.ANY),
                  pl.BlockSpec((1,) + wup.shape[1:], wmap),
                  pl.BlockSpec((1,) + wdn.shape[1:], wmap)],
        out_specs=pl.BlockSpec((tm * SUBLANES, LANES), lambda j, *_: (j, 0)),
        scratch_shapes=[pltpu.VMEM((2, tm * SUBLANES, LANES), jnp.uint32), pltpu.SemaphoreType.DMA((2,))],
    )
    return pl.pallas_call(
        functools.partial(_moe_kernel, tm=tm),
        grid_spec=grid_spec,
        out_shape=jax.ShapeDtypeStruct(((n_tiles + 1) * tm * SUBLANES, LANES), jnp.uint32),
        compiler_params=_params(),
        name="moe_experts",
    )(texp, nused, first, code, hp, wup, wdn)


def _combine_kernel(pos_ref, *refs, alpha, tm, tt, n_p, nb1, n_out):
    h_ref, ys_ref, wt_ref = refs[:3]
    p_refs = refs[3:3 + n_p]
    g_ref, b_ref, wg_ref, bg_ref, wp_ref = refs[3 + n_p:8 + n_p]
    o_refs = refs[8 + n_p:8 + n_p + n_out]
    ybuf, sem = refs[8 + n_p + n_out:]
    i = pl.program_id(0)
    slot = i % 2

    last = pl.num_programs(0) - 1
    half_rows = (range(0, tm // 2), range(tm // 2, tm))

    def gather(step, buf, k, rows):
        base = k * tt + step * tm
        for r in rows:
            src = pl.multiple_of(pos_ref[base + r], SUBLANES)
            pltpu.make_async_copy(ys_ref.at[pl.ds(src, SUBLANES)],
                                  ybuf.at[buf, k, pl.ds(SUBLANES * r, SUBLANES)], sem.at[buf]).start()

    def wait_step(buf):
        for k in range(2):
            pltpu.make_async_copy(ys_ref.at[pl.ds(0, SUBLANES * tm)], ybuf.at[buf, k], sem.at[buf]).wait()

    @pl.when(i == 0)
    def _():
        for k in range(2):
            gather(0, 0, k, range(tm))

    nxt = jnp.minimum(i + 1, last)
    wait_step(slot)
    gather(nxt, 1 - slot, 0, half_rows[0])
    wt = wt_ref[...]
    y0 = _unpack_rows(_load_token_tiles(ybuf.at[slot, 0], tm))
    y1 = _unpack_rows(_load_token_tiles(ybuf.at[slot, 1], tm))
    moe = y0 * wt[:, 0:1] + y1 * wt[:, 1:2]
    h2 = _layer_norm(alpha * h_ref[...] + moe, g_ref[...], b_ref[...])
    hb = h2.astype(BF16)
    gather(nxt, 1 - slot, 0, half_rows[1])
    half = wg_ref.shape[1] // 2
    g0 = _dot(hb, wg_ref[:, :half])
    gather(nxt, 1 - slot, 1, half_rows[0])
    g1 = _dot(hb, wg_ref[:, half:])
    gather(nxt, 1 - slot, 1, half_rows[1])
    gate = jax.nn.sigmoid(jnp.concatenate([g0, g1], axis=1) + bg_ref[...])
    x_new = h2 + gate * _dot(_read_rows(p_refs, nb1).astype(BF16), wp_ref[...])

    @pl.when(i == last)
    def _():
        wait_step(1 - slot)

    if n_out == 1:
        o_refs[0][...] = x_new
    else:
        @pl.when(i < nb1)
        def _():
            o_refs[0][...] = x_new

        @pl.when(i >= nb1)
        def _():
            o_refs[1][...] = x_new


def _combine(pos_flat, h, ys, wt, p_parts, g, b, wg, bg, wp, alpha, split_rows):
    tt, d = h.shape
    tm = min(256, min(a.shape[0] for a in p_parts))
    row = lambda i, *_: (i, 0)
    nb1 = _nb1(p_parts, tm)
    if split_rows is None:
        out_specs = pl.BlockSpec((tm, d), row)
        out_shape = jax.ShapeDtypeStruct((tt, d), F32)
        n_out = 1
    else:
        assert split_rows == p_parts[0].shape[0]
        out_specs = [pl.BlockSpec((tm, d), lambda i, *_: (jnp.minimum(i, nb1 - 1), 0)),
                     pl.BlockSpec((tm, d), lambda i, *_: (jnp.maximum(i - nb1, 0), 0))]
        out_shape = [jax.ShapeDtypeStruct((split_rows, d), F32), jax.ShapeDtypeStruct((tt - split_rows, d), F32)]
        n_out = 2
    grid_spec = pltpu.PrefetchScalarGridSpec(
        num_scalar_prefetch=1,
        grid=(tt // tm,),
        in_specs=[pl.BlockSpec((tm, d), row), pl.BlockSpec(memory_space=pl.ANY), pl.BlockSpec((tm, LANES), row)]
        + _row_specs(p_parts, tm, p_parts[0].shape[1])
        + [_resident((1, d)), _resident((1, d)), _resident(wg.shape), _resident((1, d)), _resident(wp.shape)],
        out_specs=out_specs,
        scratch_shapes=[pltpu.VMEM((2, 2, tm * SUBLANES, LANES), jnp.uint32), pltpu.SemaphoreType.DMA((2,))],
    )
    return pl.pallas_call(
        functools.partial(_combine_kernel, alpha=alpha, tm=tm, tt=tt, n_p=len(p_parts), nb1=nb1, n_out=n_out),
        grid_spec=grid_spec,
        out_shape=out_shape,
        compiler_params=_params(),
        name="moe_combine_ple",
    )(pos_flat, h, ys, wt, *p_parts, g.reshape(1, d), b.reshape(1, d), wg, bg.reshape(1, d), wp)


def _moe_block(hp, eid, wup, wdn):
    tt = eid.shape[0]
    n_tiles = (2 * tt + N_EXPERTS * (MOE_TILE - 1)) // MOE_TILE
    n_tiles_pad = -(-(n_tiles + 1) // SUBLANES) * SUBLANES
    pos, tmeta = _plan(eid, n_tiles_pad)
    code = _invert(jnp.concatenate([pos[:, 2], pos[:, 3]]), tt)
    ys = _moe(tmeta[:, 0], tmeta[0:1, 1], tmeta[:, 2], code, hp, wup, wdn, n_tiles)
    return ys, SUBLANES * jnp.concatenate([pos[:, 0], pos[:, 1]])


def kernel(x_prompt, x_sample, p_prompt, p_sample, w_in_a, sink, conv_dw_a, conv_b_a, conv_ln_g, conv_ln_b,
           w_out_a, w_in_c, conv_w_c, w_out_c, ln1_g, ln1_b, ln2_g, ln2_b, w_router_group, b_router_group,
           w_router_expert, b_router_expert, w_up, w_down, w_ple, w_ple_gate, b_ple_gate):
    bp, sp, d = x_prompt.shape
    bs, ss, _ = x_sample.shape
    depth = p_prompt.shape[0]
    tp = bp * sp
    ts = bs * ss
    dims = (tp, sp, ss)
    alpha = (2.0 * depth) ** 0.25

    x_parts = (x_prompt.reshape(tp, d), x_sample.reshape(ts, d))
    tabs = _rope_tables(max(sp, ss))
    pad = LANES - N_GROUPS - N_EXPERTS
    w_route = jnp.concatenate(
        [w_router_group, w_router_expert, jnp.zeros((depth, d, pad), F32)], axis=-1).astype(BF16)
    b_route = jnp.concatenate(
        [b_router_group, b_router_expert, jnp.zeros((depth, pad), F32)], axis=-1).reshape(depth, 1, LANES)

    for i in range(depth):
        j = i // 2
        wr, br = w_route[i], b_route[i]
        if i % 2 == 0:
            q, k, v, z = _inproj_a(x_parts, w_in_a[j].astype(BF16), tabs, dims)
            att = _attention(q, k, v, sink[j], dims)
            cv = _conv_a(z, conv_dw_a[j], conv_b_a[j], conv_ln_g[j], conv_ln_b[j], dims)
            h, hp, eid, wt = _outproj_a(att, cv, x_parts, w_out_a[j].astype(BF16), ln1_g[i], ln1_b[i], wr, br, alpha)
        else:
            bg, cx = _inproj_c(x_parts[0], w_in_c[j].astype(BF16))
            h, hp, eid, wt = _outproj_c(bg, cx, conv_w_c[j], x_parts[0], w_out_c[j].astype(BF16), ln1_g[i], ln1_b[i],
                                    wr, br, alpha, dims)
        ys, pos_flat = _moe_block(hp, eid, w_up[i].astype(BF16), w_down[i].astype(BF16))
        p_parts = (p_prompt[i].reshape(tp, -1), p_sample[i].reshape(ts, -1))
        out = _combine(pos_flat, h, ys, wt, p_parts, ln2_g[i], ln2_b[i], w_ple_gate[i].astype(BF16),
                       b_ple_gate[i], w_ple[i].astype(BF16), alpha, tp if i == depth - 1 else None)
        x_parts = (out,)

    y_prompt, y_sample = out
    return y_prompt.reshape(bp, sp, d), y_sample.reshape(bs, ss, d)
```

```python
import functools

import jax
import jax.numpy as jnp
from jax import lax
from jax.experimental import pallas as pl
from jax.experimental.pallas import tpu as pltpu

F32 = jnp.float32
BF16 = jnp.bfloat16

N_HEADS = 16
N_KV_HEADS = 4
HEAD_DIM = 64
Q_WIDTH = N_HEADS * HEAD_DIM
KV_WIDTH = N_KV_HEADS * HEAD_DIM
BLOCK = 128
ROT_DIM = HEAD_DIM // 4
ROPE_THETA = 500000.0
CONV_WIDTH = 1024
CONV_KSIZE = 31
CONV_HALO = 16
SHORT_HALO = 8
N_GROUPS = 4
EXPERTS_PER_GROUP = 8
N_EXPERTS = N_GROUPS * EXPERTS_PER_GROUP
D_EXPERT = 512
LN_EPS = 1e-5
NEG_INF = -1e30
LANES = 128
SUBLANES = 8

MOE_TILE = 256
DMA_UNROLL = 8
VMEM_LIMIT = 56 * 1024 * 1024


def _params(n_axes=1):
    return pltpu.CompilerParams(dimension_semantics=("arbitrary",) * n_axes,
                                vmem_limit_bytes=VMEM_LIMIT)


def _resident(shape):
    nd = len(shape)
    return pl.BlockSpec(shape, lambda *_: (0,) * nd, pipeline_mode=pl.Buffered(1))


def _dot(a, b):
    return jnp.dot(a, b, preferred_element_type=F32)


def _layer_norm(y, g, b):
    mu = jnp.mean(y, axis=-1, keepdims=True)
    d = y - mu
    var = jnp.mean(d * d, axis=-1, keepdims=True)
    return d * lax.rsqrt(var + LN_EPS) * g + b


def _seq_offset(r0, tp, sp, ss):
    in_p = r0 < tp
    off = jnp.where(in_p, r0 % sp, (r0 - tp) % ss)
    return off, jnp.where(in_p, sp, ss)


def _row_specs(parts, tm, width, lag=0):
    nb = sum(a.shape[0] for a in parts) // tm
    tile = lambda i: jnp.clip(i - lag, 0, nb - 1)
    if len(parts) == 1:
        return [pl.BlockSpec((tm, width), lambda i, *_: (tile(i), 0))]
    nb1 = parts[0].shape[0] // tm
    return [pl.BlockSpec((tm, width), lambda i, *_: (jnp.minimum(tile(i), nb1 - 1), 0)),
            pl.BlockSpec((tm, width), lambda i, *_: (jnp.maximum(tile(i) - nb1, 0), 0))]


def _read_rows(refs, nb1, lag=0):
    if len(refs) == 1:
        return refs[0][...]
    return jnp.where(pl.program_id(0) - lag < nb1, refs[0][...], refs[1][...])


def _nb1(parts, tm):
    return parts[0].shape[0] // tm


def _pack_rows(x):
    n = x.shape[1] // 2
    lo = lax.bitcast_convert_type(x[:, :n].astype(BF16).astype(F32), jnp.uint32) >> 16
    hi = lax.bitcast_convert_type(x[:, n:].astype(BF16).astype(F32), jnp.uint32) & jnp.uint32(0xFFFF0000)
    return lo | hi


def _unpack_rows(w):
    lo = lax.bitcast_convert_type(w << 16, F32)
    hi = lax.bitcast_convert_type(w & jnp.uint32(0xFFFF0000), F32)
    return jnp.concatenate([lo, hi], axis=1)


def _store_token_tiles(ref, w):
    m = w.shape[0]
    for s in range(SUBLANES):
        ref[pl.ds(s, m, stride=SUBLANES), :] = w[:, LANES * s:LANES * (s + 1)]


def _load_token_tiles(ref, m):
    return jnp.concatenate([ref[pl.ds(s, m, stride=SUBLANES), :] for s in range(SUBLANES)], axis=1)


def _inproj_a_kernel(*refs, n_x, nb1):
    x_refs, (w_ref, rc_ref, ra_ref, rb_ref, q_ref, k_ref, v_ref, z_ref) = refs[:n_x], refs[n_x:]
    xb = _read_rows(x_refs, nb1).astype(BF16)
    rc, ra, rb = rc_ref[...], ra_ref[...], rb_ref[...]

    def rope(u):
        outs = []
        half = ROT_DIM // 2
        for j in range(u.shape[1] // LANES):
            t = u[:, LANES * j:LANES * (j + 1)]
            outs.append(t * rc + pltpu.roll(t, LANES - half, 1) * ra + pltpu.roll(t, half, 1) * rb)
        return jnp.concatenate(outs, axis=1)

    cw = 512
    for c in range(Q_WIDTH // cw):
        u = _dot(xb, w_ref[:, cw * c:cw * (c + 1)])
        q_ref[:, cw * c:cw * (c + 1)] = (rope(u) * (HEAD_DIM ** -0.5)).astype(BF16)
    u = _dot(xb, w_ref[:, Q_WIDTH:Q_WIDTH + 2 * KV_WIDTH])
    k_ref[...] = rope(u[:, :KV_WIDTH]).astype(BF16)
    v_ref[...] = u[:, KV_WIDTH:].astype(BF16)
    base = Q_WIDTH + 2 * KV_WIDTH
    for c in range(CONV_WIDTH // cw):
        a = _dot(xb, w_ref[:, base + cw * c:base + cw * (c + 1)])
        g = _dot(xb, w_ref[:, base + CONV_WIDTH + cw * c:base + CONV_WIDTH + cw * (c + 1)])
        z_ref[:, cw * c:cw * (c + 1)] = a * jax.nn.sigmoid(g)


def _rope_tables(smax):
    half = ROT_DIM // 2
    inv_freq = ROPE_THETA ** (-(jnp.arange(half, dtype=F32) * 2.0 / ROT_DIM))
    ang = jnp.arange(smax, dtype=F32)[:, None] * inv_freq[None, :]
    cos, sin = jnp.cos(ang), jnp.sin(ang)
    d = jnp.arange(LANES) % HEAD_DIM
    f = d % half
    rc = jnp.where(d[None, :] < ROT_DIM, cos[:, f], 1.0)
    ra = jnp.where(d[None, :] < half, -sin[:, f], 0.0)
    rb = jnp.where((d[None, :] >= half) & (d[None, :] < ROT_DIM), sin[:, f], 0.0)
    return rc.astype(F32), ra.astype(F32), rb.astype(F32)


def _inproj_a(x_parts, w, tabs, dims):
    tp, sp, ss = dims
    tt = sum(a.shape[0] for a in x_parts)
    d = x_parts[0].shape[1]
    tm = min(512, sp, ss)
    n_in = w.shape[1]

    def tab_map(i):
        off, _ = _seq_offset(i * tm, tp, sp, ss)
        return (off // tm, 0)

    row = lambda i: (i, 0)
    tab_spec = pl.BlockSpec((tm, LANES), tab_map)
    return pl.pallas_call(
        functools.partial(_inproj_a_kernel, n_x=len(x_parts), nb1=_nb1(x_parts, tm)),
        grid=(tt // tm,),
        in_specs=_row_specs(x_parts, tm, d) + [_resident((d, n_in)), tab_spec, tab_spec, tab_spec],
        out_specs=[pl.BlockSpec((tm, Q_WIDTH), row), pl.BlockSpec((tm, KV_WIDTH), row),
                   pl.BlockSpec((tm, KV_WIDTH), row), pl.BlockSpec((tm, CONV_WIDTH), row)],
        out_shape=[jax.ShapeDtypeStruct((tt, Q_WIDTH), BF16), jax.ShapeDtypeStruct((tt, KV_WIDTH), BF16),
                   jax.ShapeDtypeStruct((tt, KV_WIDTH), BF16), jax.ShapeDtypeStruct((tt, CONV_WIDTH), F32)],
        compiler_params=_params(),
        name="inproj_a",
    )(*x_parts, w, *tabs)


def _attn_kernel(sink_ref, q_ref, kp_ref, kc_ref, kn_ref, vp_ref, vc_ref, vn_ref, o_ref, s_scr, p_scr, *, dims):
    tp, sp, ss = dims
    off, slen = _seq_offset(pl.program_id(0) * BLOCK, tp, sp, ss)
    no_prev = jnp.where(off > 0, 0, BLOCK)
    no_next = jnp.where(off + BLOCK < slen, 0, BLOCK)
    row = lax.broadcasted_iota(jnp.int32, (BLOCK, 3 * BLOCK), 0)
    col = lax.broadcasted_iota(jnp.int32, (BLOCK, 3 * BLOCK), 1)
    jj = col % BLOCK
    blk = col // BLOCK
    valid = ((blk == 1) | ((blk == 0) & (jj >= row + no_prev)) | ((blk == 2) & (jj + no_next <= row)))
    kcat = jnp.concatenate([kp_ref[...], kc_ref[...], kn_ref[...]], axis=0)
    vcat = jnp.concatenate([vp_ref[...], vc_ref[...], vn_ref[...]], axis=0)
    grp = N_HEADS // N_KV_HEADS

    for hh in range(N_HEADS):
        h = hh // grp
        s_scr[hh] = lax.dot_general(q_ref[:, HEAD_DIM * hh:HEAD_DIM * (hh + 1)],
                                    kcat[:, HEAD_DIM * h:HEAD_DIM * (h + 1)],
                                    (((1,), (1,)), ((), ())), preferred_element_type=F32)
    sk = sink_ref[:, 0:1, 0:1]
    s = jnp.where(valid[None], s_scr[...], NEG_INF)
    m = jnp.maximum(jnp.max(s, axis=-1, keepdims=True), sk)
    p_scr[...] = jnp.exp(s - m).astype(BF16)
    sink_term = jnp.exp(sk - m)
    ones = jnp.ones((3 * BLOCK, HEAD_DIM), BF16)
    per_vreg = LANES // HEAD_DIM
    for c in range(N_HEADS // per_vreg):
        outs = []
        for hh in range(per_vreg * c, per_vreg * (c + 1)):
            h = hh // grp
            p = p_scr[hh]
            denom = _dot(p, ones) + sink_term[hh]
            outs.append(_dot(p, vcat[:, HEAD_DIM * h:HEAD_DIM * (h + 1)]) / denom)
        o_ref[:, LANES * c:LANES * (c + 1)] = jnp.concatenate(outs, axis=1).astype(BF16)


def _attention(q, k, v, sink, dims):
    tt = q.shape[0]
    nb = tt // BLOCK
    cur = lambda i: (i, 0)
    prv = lambda i: (jnp.maximum(i - 1, 0), 0)
    nxt = lambda i: (jnp.minimum(i + 1, nb - 1), 0)
    kv = lambda m: pl.BlockSpec((BLOCK, KV_WIDTH), m)
    sink_b = jnp.broadcast_to(sink.astype(F32)[:, None, None], (N_HEADS, SUBLANES, LANES))
    return pl.pallas_call(
        functools.partial(_attn_kernel, dims=dims),
        grid=(nb,),
        in_specs=[_resident(sink_b.shape), pl.BlockSpec((BLOCK, Q_WIDTH), cur),
                  kv(prv), kv(cur), kv(nxt), kv(prv), kv(cur), kv(nxt)],
        out_specs=pl.BlockSpec((BLOCK, Q_WIDTH), cur),
        out_shape=jax.ShapeDtypeStruct((tt, Q_WIDTH), BF16),
        scratch_shapes=[pltpu.VMEM((N_HEADS, BLOCK, 3 * BLOCK), F32),
                        pltpu.VMEM((N_HEADS, BLOCK, 3 * BLOCK), BF16)],
        compiler_params=_params(),
        name="band_attention",
    )(sink_b, q, k, k, k, v, v, v)


def _conv_a_kernel(z_ref, zp_ref, zn_ref, w_ref, b_ref, g_ref, beta_ref, o_ref, ext_ref, *, dims, ts):
    tp, sp, ss = dims
    off, slen = _seq_offset(pl.program_id(0) * ts, tp, sp, ss)
    keep_prev = (off > 0).astype(F32)
    keep_next = (off + ts < slen).astype(F32)
    ext_ref[0:CONV_HALO, :] = zp_ref[...] * keep_prev
    ext_ref[CONV_HALO:CONV_HALO + ts, :] = z_ref[...]
    ext_ref[CONV_HALO + ts:, :] = zn_ref[...] * keep_next
    rows, lw = 32, 256
    n_a = -(-CONV_KSIZE // SUBLANES)
    for rc in range(ts // rows):
        r0 = rows * rc
        accs = []
        for lc in range(CONV_WIDTH // lw):
            ls = slice(lw * lc, lw * (lc + 1))
            acc = jnp.zeros((rows, lw), F32)
            for b in range(SUBLANES):
                part = jnp.zeros((rows + SUBLANES, lw), F32)
                for a in range(n_a):
                    k = SUBLANES * a + b
                    if k < CONV_KSIZE:
                        lo = r0 + SUBLANES * a
                        part = part + ext_ref[lo:lo + rows + SUBLANES, ls] * w_ref[k:k + 1, ls]
                acc = acc + part[b + 1:b + 1 + rows, :]
            accs.append(acc + b_ref[:, ls])
        y = _layer_norm(jnp.concatenate(accs, axis=1), g_ref[...], beta_ref[...])
        o_ref[r0:r0 + rows, :] = (y * jax.nn.sigmoid(y)).astype(BF16)


def _conv_a(z, w, b, g, beta, dims):
    tt, cw = z.shape
    tp, sp, ss = dims
    ts = min(256, sp, ss)
    hb = ts // CONV_HALO
    nh = tt // CONV_HALO
    vec = lambda a: a.reshape(1, cw)
    return pl.pallas_call(
        functools.partial(_conv_a_kernel, dims=dims, ts=ts),
        grid=(tt // ts,),
        in_specs=[pl.BlockSpec((ts, cw), lambda i: (i, 0)),
                  pl.BlockSpec((CONV_HALO, cw), lambda i: (jnp.maximum(i * hb - 1, 0), 0)),
                  pl.BlockSpec((CONV_HALO, cw), lambda i: (jnp.minimum((i + 1) * hb, nh - 1), 0)),
                  _resident((CONV_KSIZE, cw)), _resident((1, cw)), _resident((1, cw)), _resident((1, cw))],
        out_specs=pl.BlockSpec((ts, cw), lambda i: (i, 0)),
        out_shape=jax.ShapeDtypeStruct((tt, cw), BF16),
        scratch_shapes=[pltpu.VMEM((ts + 2 * CONV_HALO, cw), F32)],
        compiler_params=_params(),
        name="conformer_conv",
    )(z, z, z, w, vec(b), vec(g), vec(beta))


def _norm_and_route(mix, x, g_ref, b_ref, wr_ref, br_ref, outs, alpha):
    h_ref, hp_ref, eid_ref, wt_ref = outs
    h = _layer_norm(alpha * x + mix, g_ref[...], b_ref[...])
    h_ref[...] = h
    _store_token_tiles(hp_ref, _pack_rows(h))
    logits = _dot(h.astype(BF16), wr_ref[...]) + br_ref[...]
    lane = lax.broadcasted_iota(jnp.int32, logits.shape, 1).astype(F32)
    big = jnp.float32(LANES)
    ninf = jnp.float32(-jnp.inf)
    gmask = lane < N_GROUPS
    lg = jnp.where(gmask, logits, ninf)
    gmax = jnp.max(lg, axis=-1, keepdims=True)
    gidx = jnp.min(jnp.where(lg == gmax, lane, big), axis=-1, keepdims=True)
    gsum = jnp.sum(jnp.where(gmask, jnp.exp(logits - gmax), 0.0), axis=-1, keepdims=True)
    lo = N_GROUPS + EXPERTS_PER_GROUP * gidx
    le = jnp.where((lane >= lo) & (lane < lo + EXPERTS_PER_GROUP), logits, ninf)
    t0 = jnp.max(le, axis=-1, keepdims=True)
    i0 = jnp.min(jnp.where(le == t0, lane, big), axis=-1, keepdims=True)
    le1 = jnp.where(lane == i0, ninf, le)
    t1 = jnp.max(le1, axis=-1, keepdims=True)
    i1 = jnp.min(jnp.where(le1 == t1, lane, big), axis=-1, keepdims=True)
    r = jnp.exp(t1 - t0)
    w0 = 1.0 / ((1.0 + r) * gsum)
    w1 = r * w0
    eid_ref[...] = jnp.where(lane == 0, i0 - N_GROUPS, jnp.where(lane == 1, i1 - N_GROUPS, 0.0)).astype(jnp.int32)
    wt_ref[...] = jnp.where(lane == 0, w0, jnp.where(lane == 1, w1, 0.0))


def _outproj_a_kernel(*refs, alpha, n_x, nb1):
    att_ref, cv_ref = refs[:2]
    x_refs = refs[2:2 + n_x]
    w_ref, g_ref, b_ref, wr_ref, br_ref = refs[2 + n_x:7 + n_x]
    mix = _dot(att_ref[...], w_ref[0:Q_WIDTH, :]) + _dot(cv_ref[...], w_ref[Q_WIDTH:, :])
    _norm_and_route(mix, _read_rows(x_refs, nb1), g_ref, b_ref, wr_ref, br_ref, refs[7 + n_x:], alpha)


def _route_outs(tt, d, tm, lag=0):
    nb = tt // tm
    row = lambda i: (jnp.clip(i - lag, 0, nb - 1), 0)
    specs = [pl.BlockSpec((tm, d), row), pl.BlockSpec((tm * SUBLANES, LANES), row),
             pl.BlockSpec((tm, LANES), row), pl.BlockSpec((tm, LANES), row)]
    shapes = [jax.ShapeDtypeStruct((tt, d), F32), jax.ShapeDtypeStruct((tt * SUBLANES, LANES), jnp.uint32),
              jax.ShapeDtypeStruct((tt, LANES), jnp.int32), jax.ShapeDtypeStruct((tt, LANES), F32)]
    return specs, shapes


def _outproj_a(att, cv, x_parts, w, g, b, wr, br, alpha):
    tt = att.shape[0]
    d = w.shape[1]
    tm = min(512, min(a.shape[0] for a in x_parts))
    row = lambda i: (i, 0)
    out_specs, out_shape = _route_outs(tt, d, tm)
    return pl.pallas_call(
        functools.partial(_outproj_a_kernel, alpha=alpha, n_x=len(x_parts), nb1=_nb1(x_parts, tm)),
        grid=(tt // tm,),
        in_specs=[pl.BlockSpec((tm, Q_WIDTH), row), pl.BlockSpec((tm, CONV_WIDTH), row)]
        + _row_specs(x_parts, tm, d)
        + [_resident(w.shape), _resident((1, d)), _resident((1, d)), _resident(wr.shape), _resident((1, LANES))],
        out_specs=out_specs, out_shape=out_shape,
        compiler_params=_params(),
        name="outproj_a",
    )(att, cv, *x_parts, w, g.reshape(1, d), b.reshape(1, d), wr, br)


def _inproj_c_kernel(x_ref, w_ref, bg_ref, cx_ref):
    xb = x_ref[...].astype(BF16)
    d = x_ref.shape[1]
    cw = 512
    for c in range(d // cw):
        cs = slice(cw * c, cw * (c + 1))
        bg_ref[:, cs] = _dot(xb, w_ref[:, cw * c:cw * (c + 1)])
        cg = _dot(xb, w_ref[:, d + cw * c:d + cw * (c + 1)])
        xv = _dot(xb, w_ref[:, 2 * d + cw * c:2 * d + cw * (c + 1)])
        cx_ref[:, cs] = cg * xv


def _inproj_c(x, w):
    tt, d = x.shape
    tm = min(256, tt)
    row = lambda i: (i, 0)
    return pl.pallas_call(
        _inproj_c_kernel,
        grid=(tt // tm,),
        in_specs=[pl.BlockSpec((tm, d), row), _resident(w.shape)],
        out_specs=[pl.BlockSpec((tm, d), row), pl.BlockSpec((tm, d), row)],
        out_shape=[jax.ShapeDtypeStruct((tt, d), F32), jax.ShapeDtypeStruct((tt, d), F32)],
        compiler_params=_params(),
        name="inproj_c",
    )(x, w)


def _outproj_c_kernel(bg_ref, cx_ref, cp_ref, cn_ref, cw_ref, x_ref, w_ref, g_ref, b_ref, wr_ref, br_ref,
                      *outs, alpha, dims, tm):
    tp, sp, ss = dims
    off, slen = _seq_offset(pl.program_id(0) * tm, tp, sp, ss)
    keep_prev = (off > 0).astype(F32)
    keep_next = (off + tm < slen).astype(F32)
    cx = cx_ref[...]
    first = cp_ref[SHORT_HALO - 1:SHORT_HALO, :] * keep_prev
    last = cn_ref[0:1, :] * keep_next
    rid = lax.broadcasted_iota(jnp.int32, cx.shape, 0)
    before = jnp.where(rid == 0, first, pltpu.roll(cx, 1, 0))
    after = jnp.where(rid == tm - 1, last, pltpu.roll(cx, tm - 1, 0))
    conv = before * cw_ref[0:1, :] + cx * cw_ref[1:2, :] + after * cw_ref[2:3, :]
    y = (bg_ref[...] * conv).astype(BF16)
    _norm_and_route(_dot(y, w_ref[...]), x_ref[...], g_ref, b_ref, wr_ref, br_ref, outs, alpha)


def _outproj_c(bg, cx, cw, x, w, g, b, wr, br, alpha, dims):
    tt, d = x.shape
    tp, sp, ss = dims
    tm = min(512, sp, ss)
    hb = tm // SHORT_HALO
    nh = tt // SHORT_HALO
    row = lambda i: (i, 0)
    out_specs, out_shape = _route_outs(tt, d, tm)
    return pl.pallas_call(
        functools.partial(_outproj_c_kernel, alpha=alpha, dims=dims, tm=tm),
        grid=(tt // tm,),
        in_specs=[pl.BlockSpec((tm, d), row), pl.BlockSpec((tm, d), row),
                  pl.BlockSpec((SHORT_HALO, d), lambda i: (jnp.maximum(i * hb - 1, 0), 0)),
                  pl.BlockSpec((SHORT_HALO, d), lambda i: (jnp.minimum((i + 1) * hb, nh - 1), 0)),
                  _resident(cw.shape), pl.BlockSpec((tm, d), row), _resident(w.shape),
                  _resident((1, d)), _resident((1, d)), _resident(wr.shape), _resident((1, LANES))],
        out_specs=out_specs, out_shape=out_shape,
        compiler_params=_params(),
        name="outproj_c",
    )(bg, cx, cx, cx, cw, x, w, g.reshape(1, d), b.reshape(1, d), wr, br)


def _plan_kernel(eid_ref, pos_ref, tmeta_ref, pbase_ref, cbase_ref, *, tile, n_tiles_pad, n_codes):
    phase = pl.program_id(0)
    t = pl.program_id(1)
    tp = eid_ref.shape[0]
    lane = lax.broadcasted_iota(jnp.int32, (tp, LANES), 1)
    oh0 = lane == eid_ref[:, 0:1]
    oh1 = lane == eid_ref[:, 1:2]
    chosen = jnp.where(oh0 | oh1, 1.0, 0.0)
    counts = jnp.sum(chosen, axis=0, keepdims=True)

    @pl.when((phase == 0) & (t == 0))
    def _():
        pbase_ref[...] = jnp.zeros_like(pbase_ref)

    @pl.when(phase == 0)
    def _():
        pbase_ref[...] += counts

    @pl.when((phase == 1) & (t == 0))
    def _():
        totals = pbase_ref[...]
        ntile = jnp.floor((totals + (tile - 1)) * (1.0 / tile))
        hi = jnp.floor(totals * (1.0 / 256.0))
        lo = totals - 256.0 * hi
        r = lax.broadcasted_iota(jnp.int32, (LANES, LANES), 0)
        c = lax.broadcasted_iota(jnp.int32, (LANES, LANES), 1)
        upper = jnp.where(r < c, 1.0, 0.0).astype(BF16)

        def prefix(v):
            return _dot(jnp.broadcast_to(v, (SUBLANES, LANES)).astype(BF16), upper)[0:1, :]

        tstart = prefix(ntile)
        cstart = 256.0 * prefix(hi) + prefix(lo)
        pbase_ref[...] = tstart * tile
        cbase_ref[...] = cstart
        j = lax.broadcasted_iota(jnp.int32, (n_tiles_pad, LANES), 0).astype(F32)
        ln = lax.broadcasted_iota(jnp.int32, (n_tiles_pad, LANES), 1)
        done = jnp.where((tstart + ntile <= j) & (ln < N_EXPERTS), 1.0, 0.0)
        texp = jnp.minimum(jnp.sum(done, axis=-1, keepdims=True), N_EXPERTS - 1.0)
        total = jnp.sum(jnp.where(ln[0:1, :] < N_EXPERTS, ntile, 0.0), axis=-1, keepdims=True)
        mine = ln.astype(F32) == texp
        pick = lambda v: jnp.sum(jnp.where(mine, v, 0.0), axis=-1, keepdims=True)
        first = pick(cstart) + (j[:, 0:1] - pick(tstart)) * tile
        first = jnp.clip(first, 0.0, n_codes - 1.0)
        tmeta_ref[...] = jnp.where(ln == 1, total, jnp.where(ln == 2, first, texp)).astype(jnp.int32)

    @pl.when(phase == 1)
    def _():
        r = lax.broadcasted_iota(jnp.int32, (tp, tp), 0)
        c = lax.broadcasted_iota(jnp.int32, (tp, tp), 1)
        lower = jnp.where(c < r, 1.0, 0.0).astype(BF16)
        rank = _dot(lower, chosen.astype(BF16))
        pslot = pbase_ref[...] + rank
        cslot = cbase_ref[...] + rank
        pick = lambda oh, v: jnp.sum(jnp.where(oh, v, 0.0), axis=-1, keepdims=True)
        out = jnp.where(lane == 0, pick(oh0, pslot), jnp.where(lane == 1, pick(oh1, pslot),
              jnp.where(lane == 2, pick(oh0, cslot), jnp.where(lane == 3, pick(oh1, cslot), 0.0))))
        pos_ref[...] = out.astype(jnp.int32)
        pbase_ref[...] += counts
        cbase_ref[...] += counts


def _plan(eid, n_tiles_pad):
    tt = eid.shape[0]
    tp = min(512, tt)
    return pl.pallas_call(
        functools.partial(_plan_kernel, tile=MOE_TILE, n_tiles_pad=n_tiles_pad, n_codes=2 * tt),
        grid=(2, tt // tp),
        in_specs=[pl.BlockSpec((tp, LANES), lambda ph, t: (t, 0))],
        out_specs=[pl.BlockSpec((tp, LANES), lambda ph, t: (t * ph, 0)),
                   pl.BlockSpec((n_tiles_pad, LANES), lambda ph, t: (0, 0))],
        out_shape=[jax.ShapeDtypeStruct((tt, LANES), jnp.int32),
                   jax.ShapeDtypeStruct((n_tiles_pad, LANES), jnp.int32)],
        scratch_shapes=[pltpu.VMEM((1, LANES), F32), pltpu.VMEM((1, LANES), F32)],
        compiler_params=_params(2),
        name="moe_plan",
    )(eid)


def _invert_kernel(cpos_ref, code_ref, *, tt, tok):
    step = pl.program_id(0)

    @pl.when(step == 0)
    def _():
        for r in range(MOE_TILE):
            code_ref[2 * tt + r] = 0

    def put(r, _):
        t = step * tok + r
        code_ref[cpos_ref[t]] = SUBLANES * t
        code_ref[cpos_ref[tt + t]] = SUBLANES * t
        return 0

    lax.fori_loop(0, tok, put, 0, unroll=DMA_UNROLL)


def _invert(cpos_flat, tt):
    tok = min(512, tt)
    grid_spec = pltpu.PrefetchScalarGridSpec(
        num_scalar_prefetch=1, grid=(tt // tok,), in_specs=[],
        out_specs=pl.BlockSpec(memory_space=pltpu.SMEM))
    return pl.pallas_call(
        functools.partial(_invert_kernel, tt=tt, tok=tok),
        grid_spec=grid_spec,
        out_shape=jax.ShapeDtypeStruct((2 * tt + MOE_TILE,), jnp.int32),
        compiler_params=_params(),
        name="moe_invert",
    )(cpos_flat)


def _moe_kernel(texp_ref, nused_ref, first_ref, code_ref, hp_ref, wup_ref, wdn_ref, o_ref,
                xbuf, gsem, wup_b, wdn_b, *, tm):
    j = pl.program_id(0)
    nused = nused_ref[0]
    slot = j % 2

    def gather(tile, buf, rows):
        first = first_ref[tile]
        for r in rows:
            src = pl.multiple_of(code_ref[first + r], SUBLANES)
            pltpu.make_async_copy(hp_ref.at[pl.ds(src, SUBLANES)],
                                  xbuf.at[buf, pl.ds(SUBLANES * r, SUBLANES)], gsem.at[buf]).start()

    def wait_tile(buf):
        pltpu.make_async_copy(hp_ref.at[pl.ds(0, SUBLANES * tm)], xbuf.at[buf], gsem.at[buf]).wait()

    @pl.when(j == 0)
    def _():
        gather(0, 0, range(tm))

    @pl.when(j < nused)
    def _():
        gather(j + 1, 1 - slot, range(tm))

    @pl.when((j < nused) & ((j == 0) | (texp_ref[j] != texp_ref[jnp.maximum(j - 1, 0)])))
    def _():
        wup_b[...] = wup_ref[0].astype(BF16)
        wdn_b[...] = wdn_ref[0].astype(BF16)

    @pl.when(j < nused)
    def _():
        wait_tile(slot)
        xb = _unpack_rows(_load_token_tiles(xbuf.at[slot], tm)).astype(BF16)
        up = _dot(xb, wup_b[...])
        a = up[:, :D_EXPERT]
        hid = (a * jax.nn.sigmoid(a) * up[:, D_EXPERT:]).astype(BF16)
        _store_token_tiles(o_ref, _pack_rows(_dot(hid, wdn_b[...])))

    @pl.when(j == nused)
    def _():
        wait_tile(slot)

    @pl.when(j >= nused)
    def _():
        o_ref[...] = jnp.zeros_like(o_ref)


def _moe(texp, nused, first, code, hp, wup, wdn, n_tiles, layer):
    tm = MOE_TILE

    def wmap(j, texp, nused, first, code):
        return (layer * N_EXPERTS + texp[jnp.minimum(j, nused[0] - 1)], 0, 0)

    grid_spec = pltpu.PrefetchScalarGridSpec(
        num_scalar_prefetch=4,
        grid=(n_tiles + 1,),
        in_specs=[pl.BlockSpec(memory_space=pl.ANY),
                  pl.BlockSpec((1,) + wup.shape[1:], wmap),
                  pl.BlockSpec((1,) + wdn.shape[1:], wmap)],
        out_specs=pl.BlockSpec((tm * SUBLANES, LANES), lambda j, *_: (j, 0)),
        scratch_shapes=[pltpu.VMEM((2, tm * SUBLANES, LANES), jnp.uint32), pltpu.SemaphoreType.DMA((2,)),
                        pltpu.VMEM(wup.shape[1:], BF16), pltpu.VMEM(wdn.shape[1:], BF16)],
    )
    return pl.pallas_call(
        functools.partial(_moe_kernel, tm=tm),
        grid_spec=grid_spec,
        out_shape=jax.ShapeDtypeStruct(((n_tiles + 1) * tm * SUBLANES, LANES), jnp.uint32),
        compiler_params=_params(),
        name="moe_experts",
    )(texp, nused, first, code, hp, wup, wdn)


def _combine_kernel(pos_ref, *refs, alpha, tm, tt, n_p, nb1, n_out):
    h_ref, ys_ref, wt_ref = refs[:3]
    p_refs = refs[3:3 + n_p]
    g_ref, b_ref, wg_ref, bg_ref, wp_ref = refs[3 + n_p:8 + n_p]
    o_refs = refs[8 + n_p:8 + n_p + n_out]
    ybuf, sem = refs[8 + n_p + n_out:]
    i = pl.program_id(0)
    slot = i % 2

    last = pl.num_programs(0) - 1

    def gather(step, buf, k, rows):
        base = k * tt + step * tm
        for r in rows:
            src = pl.multiple_of(pos_ref[base + r], SUBLANES)
            pltpu.make_async_copy(ys_ref.at[pl.ds(src, SUBLANES)],
                                  ybuf.at[buf, k, pl.ds(SUBLANES * r, SUBLANES)], sem.at[buf]).start()

    def wait_step(buf):
        for k in range(2):
            pltpu.make_async_copy(ys_ref.at[pl.ds(0, SUBLANES * tm)], ybuf.at[buf, k], sem.at[buf]).wait()

    @pl.when(i == 0)
    def _():
        for k in range(2):
            gather(0, 0, k, range(tm))

    @pl.when(i < last)
    def _():
        for k in range(2):
            gather(i + 1, 1 - slot, k, range(tm))

    wait_step(slot)
    wt = wt_ref[...]
    y0 = _unpack_rows(_load_token_tiles(ybuf.at[slot, 0], tm))
    y1 = _unpack_rows(_load_token_tiles(ybuf.at[slot, 1], tm))
    moe = y0 * wt[:, 0:1] + y1 * wt[:, 1:2]
    h2 = _layer_norm(alpha * h_ref[...] + moe, g_ref[...], b_ref[...])
    gate = jax.nn.sigmoid(_dot(h2.astype(BF16), wg_ref[...]) + bg_ref[...])
    x_new = h2 + gate * _dot(_read_rows(p_refs, nb1).astype(BF16), wp_ref[...])

    if n_out == 1:
        o_refs[0][...] = x_new
    else:
        @pl.when(i < nb1)
        def _():
            o_refs[0][...] = x_new

        @pl.when(i >= nb1)
        def _():
            o_refs[1][...] = x_new


def _combine(pos_flat, h, ys, wt, p_parts, g, b, wg, bg, wp, alpha, split_rows):
    tt, d = h.shape
    tm = min(256, min(a.shape[0] for a in p_parts))
    row = lambda i, *_: (i, 0)
    nb1 = _nb1(p_parts, tm)
    if split_rows is None:
        out_specs = pl.BlockSpec((tm, d), row)
        out_shape = jax.ShapeDtypeStruct((tt, d), F32)
        n_out = 1
    else:
        assert split_rows == p_parts[0].shape[0]
        out_specs = [pl.BlockSpec((tm, d), lambda i, *_: (jnp.minimum(i, nb1 - 1), 0)),
                     pl.BlockSpec((tm, d), lambda i, *_: (jnp.maximum(i - nb1, 0), 0))]
        out_shape = [jax.ShapeDtypeStruct((split_rows, d), F32), jax.ShapeDtypeStruct((tt - split_rows, d), F32)]
        n_out = 2
    grid_spec = pltpu.PrefetchScalarGridSpec(
        num_scalar_prefetch=1,
        grid=(tt // tm,),
        in_specs=[pl.BlockSpec((tm, d), row), pl.BlockSpec(memory_space=pl.ANY), pl.BlockSpec((tm, LANES), row)]
        + _row_specs(p_parts, tm, p_parts[0].shape[1])
        + [_resident((1, d)), _resident((1, d)), _resident(wg.shape), _resident((1, d)), _resident(wp.shape)],
        out_specs=out_specs,
        scratch_shapes=[pltpu.VMEM((2, 2, tm * SUBLANES, LANES), jnp.uint32), pltpu.SemaphoreType.DMA((2,))],
    )
    return pl.pallas_call(
        functools.partial(_combine_kernel, alpha=alpha, tm=tm, tt=tt, n_p=len(p_parts), nb1=nb1, n_out=n_out),
        grid_spec=grid_spec,
        out_shape=out_shape,
        compiler_params=_params(),
        name="moe_combine_ple",
    )(pos_flat, h, ys, wt, *p_parts, g.reshape(1, d), b.reshape(1, d), wg, bg.reshape(1, d), wp)


def _moe_block(hp, eid, wup, wdn, layer):
    tt = eid.shape[0]
    n_tiles = (2 * tt + N_EXPERTS * (MOE_TILE - 1)) // MOE_TILE
    n_tiles_pad = -(-(n_tiles + 1) // SUBLANES) * SUBLANES
    pos, tmeta = _plan(eid, n_tiles_pad)
    code = _invert(jnp.concatenate([pos[:, 2], pos[:, 3]]), tt)
    ys = _moe(tmeta[:, 0], tmeta[0:1, 1], tmeta[:, 2], code, hp, wup, wdn, n_tiles, layer)
    return ys, SUBLANES * jnp.concatenate([pos[:, 0], pos[:, 1]])


def kernel(x_prompt, x_sample, p_prompt, p_sample, w_in_a, sink, conv_dw_a, conv_b_a, conv_ln_g, conv_ln_b,
           w_out_a, w_in_c, conv_w_c, w_out_c, ln1_g, ln1_b, ln2_g, ln2_b, w_router_group, b_router_group,
           w_router_expert, b_router_expert, w_up, w_down, w_ple, w_ple_gate, b_ple_gate):
    bp, sp, d = x_prompt.shape
    bs, ss, _ = x_sample.shape
    depth = p_prompt.shape[0]
    tp = bp * sp
    ts = bs * ss
    dims = (tp, sp, ss)
    alpha = (2.0 * depth) ** 0.25

    x_parts = (x_prompt.reshape(tp, d), x_sample.reshape(ts, d))
    tabs = _rope_tables(max(sp, ss))
    pad = LANES - N_GROUPS - N_EXPERTS
    w_route = jnp.concatenate(
        [w_router_group, w_router_expert, jnp.zeros((depth, d, pad), F32)], axis=-1).astype(BF16)
    b_route = jnp.concatenate(
        [b_router_group, b_router_expert, jnp.zeros((depth, pad), F32)], axis=-1).reshape(depth, 1, LANES)

    w_up_all = w_up.reshape((depth * N_EXPERTS,) + w_up.shape[2:])
    w_down_all = w_down.reshape((depth * N_EXPERTS,) + w_down.shape[2:])

    for i in range(depth):
        j = i // 2
        wr, br = w_route[i], b_route[i]
        if i % 2 == 0:
            q, k, v, z = _inproj_a(x_parts, w_in_a[j].astype(BF16), tabs, dims)
            att = _attention(q, k, v, sink[j], dims)
            cv = _conv_a(z, conv_dw_a[j], conv_b_a[j], conv_ln_g[j], conv_ln_b[j], dims)
            h, hp, eid, wt = _outproj_a(att, cv, x_parts, w_out_a[j].astype(BF16), ln1_g[i], ln1_b[i], wr, br, alpha)
        else:
            bg, cx = _inproj_c(x_parts[0], w_in_c[j].astype(BF16))
            h, hp, eid, wt = _outproj_c(bg, cx, conv_w_c[j], x_parts[0], w_out_c[j].astype(BF16), ln1_g[i], ln1_b[i],
                                    wr, br, alpha, dims)
        ys, pos_flat = _moe_block(hp, eid, w_up_all, w_down_all, i)
        p_parts = (p_prompt[i].reshape(tp, -1), p_sample[i].reshape(ts, -1))
        out = _combine(pos_flat, h, ys, wt, p_parts, ln2_g[i], ln2_b[i], w_ple_gate[i].astype(BF16),
                       b_ple_gate[i], w_ple[i].astype(BF16), alpha, tp if i == depth - 1 else None)
        x_parts = (out,)

    y_prompt, y_sample = out
    return y_prompt.reshape(bp, sp, d), y_sample.reshape(bs, ss, d)
```

```python
import functools

import jax
import jax.numpy as jnp
from jax import lax
from jax.experimental import pallas as pl
from jax.experimental.pallas import tpu as pltpu

F32 = jnp.float32
BF16 = jnp.bfloat16

N_HEADS = 16
N_KV_HEADS = 4
HEAD_DIM = 64
Q_WIDTH = N_HEADS * HEAD_DIM
KV_WIDTH = N_KV_HEADS * HEAD_DIM
BLOCK = 128
ROT_DIM = HEAD_DIM // 4
ROPE_THETA = 500000.0
CONV_WIDTH = 1024
CONV_KSIZE = 31
CONV_HALO = 16
SHORT_HALO = 8
N_GROUPS = 4
EXPERTS_PER_GROUP = 8
N_EXPERTS = N_GROUPS * EXPERTS_PER_GROUP
D_EXPERT = 512
LN_EPS = 1e-5
NEG_INF = -1e30
LANES = 128
SUBLANES = 8

MOE_TILE = 256
DMA_UNROLL = 8
VMEM_LIMIT = 56 * 1024 * 1024


def _params(n_axes=1):
    return pltpu.CompilerParams(dimension_semantics=("arbitrary",) * n_axes,
                                vmem_limit_bytes=VMEM_LIMIT)


def _resident(shape):
    nd = len(shape)
    return pl.BlockSpec(shape, lambda *_: (0,) * nd, pipeline_mode=pl.Buffered(1))


def _dot(a, b):
    return jnp.dot(a, b, preferred_element_type=F32)


def _layer_norm(y, g, b):
    mu = jnp.mean(y, axis=-1, keepdims=True)
    d = y - mu
    var = jnp.mean(d * d, axis=-1, keepdims=True)
    return d * lax.rsqrt(var + LN_EPS) * g + b


def _seq_offset(r0, tp, sp, ss):
    in_p = r0 < tp
    off = jnp.where(in_p, r0 % sp, (r0 - tp) % ss)
    return off, jnp.where(in_p, sp, ss)


def _row_specs(parts, tm, width, lag=0):
    nb = sum(a.shape[0] for a in parts) // tm
    tile = lambda i: jnp.clip(i - lag, 0, nb - 1)
    if len(parts) == 1:
        return [pl.BlockSpec((tm, width), lambda i, *_: (tile(i), 0))]
    nb1 = parts[0].shape[0] // tm
    return [pl.BlockSpec((tm, width), lambda i, *_: (jnp.minimum(tile(i), nb1 - 1), 0)),
            pl.BlockSpec((tm, width), lambda i, *_: (jnp.maximum(tile(i) - nb1, 0), 0))]


def _read_rows(refs, nb1, lag=0):
    if len(refs) == 1:
        return refs[0][...]
    return jnp.where(pl.program_id(0) - lag < nb1, refs[0][...], refs[1][...])


def _nb1(parts, tm):
    return parts[0].shape[0] // tm


def _pack_rows(x):
    n = x.shape[1] // 2
    lo = lax.bitcast_convert_type(x[:, :n].astype(BF16).astype(F32), jnp.uint32) >> 16
    hi = lax.bitcast_convert_type(x[:, n:].astype(BF16).astype(F32), jnp.uint32) & jnp.uint32(0xFFFF0000)
    return lo | hi


def _unpack_rows(w):
    lo = lax.bitcast_convert_type(w << 16, F32)
    hi = lax.bitcast_convert_type(w & jnp.uint32(0xFFFF0000), F32)
    return jnp.concatenate([lo, hi], axis=1)


def _store_token_tiles(ref, w):
    m = w.shape[0]
    for s in range(SUBLANES):
        ref[pl.ds(s, m, stride=SUBLANES), :] = w[:, LANES * s:LANES * (s + 1)]


def _load_token_tiles(ref, m):
    return jnp.concatenate([ref[pl.ds(s, m, stride=SUBLANES), :] for s in range(SUBLANES)], axis=1)


def _inproj_a_kernel(*refs, n_x, nb1):
    x_refs, (w_ref, rc_ref, ra_ref, rb_ref, q_ref, k_ref, v_ref, z_ref) = refs[:n_x], refs[n_x:]
    xb = _read_rows(x_refs, nb1).astype(BF16)
    rc, ra, rb = rc_ref[...], ra_ref[...], rb_ref[...]

    def rope(u):
        outs = []
        half = ROT_DIM // 2
        for j in range(u.shape[1] // LANES):
            t = u[:, LANES * j:LANES * (j + 1)]
            outs.append(t * rc + pltpu.roll(t, LANES - half, 1) * ra + pltpu.roll(t, half, 1) * rb)
        return jnp.concatenate(outs, axis=1)

    cw = 512
    for c in range(Q_WIDTH // cw):
        u = _dot(xb, w_ref[:, cw * c:cw * (c + 1)])
        q_ref[cw * c:cw * (c + 1), :] = (rope(u) * (HEAD_DIM ** -0.5)).T.astype(BF16)
    u = _dot(xb, w_ref[:, Q_WIDTH:Q_WIDTH + 2 * KV_WIDTH])
    k_ref[...] = rope(u[:, :KV_WIDTH]).astype(BF16)
    v_ref[...] = u[:, KV_WIDTH:].T.astype(BF16)
    base = Q_WIDTH + 2 * KV_WIDTH
    for c in range(CONV_WIDTH // cw):
        a = _dot(xb, w_ref[:, base + cw * c:base + cw * (c + 1)])
        g = _dot(xb, w_ref[:, base + CONV_WIDTH + cw * c:base + CONV_WIDTH + cw * (c + 1)])
        z_ref[:, cw * c:cw * (c + 1)] = a * jax.nn.sigmoid(g)


def _rope_tables(smax):
    half = ROT_DIM // 2
    inv_freq = ROPE_THETA ** (-(jnp.arange(half, dtype=F32) * 2.0 / ROT_DIM))
    ang = jnp.arange(smax, dtype=F32)[:, None] * inv_freq[None, :]
    cos, sin = jnp.cos(ang), jnp.sin(ang)
    d = jnp.arange(LANES) % HEAD_DIM
    f = d % half
    rc = jnp.where(d[None, :] < ROT_DIM, cos[:, f], 1.0)
    ra = jnp.where(d[None, :] < half, -sin[:, f], 0.0)
    rb = jnp.where((d[None, :] >= half) & (d[None, :] < ROT_DIM), sin[:, f], 0.0)
    return rc.astype(F32), ra.astype(F32), rb.astype(F32)


def _inproj_a(x_parts, w, tabs, dims):
    tp, sp, ss = dims
    tt = sum(a.shape[0] for a in x_parts)
    d = x_parts[0].shape[1]
    tm = min(512, sp, ss)
    n_in = w.shape[1]

    def tab_map(i):
        off, _ = _seq_offset(i * tm, tp, sp, ss)
        return (off // tm, 0)

    row = lambda i: (i, 0)
    col = lambda i: (0, i)
    tab_spec = pl.BlockSpec((tm, LANES), tab_map)
    return pl.pallas_call(
        functools.partial(_inproj_a_kernel, n_x=len(x_parts), nb1=_nb1(x_parts, tm)),
        grid=(tt // tm,),
        in_specs=_row_specs(x_parts, tm, d) + [_resident((d, n_in)), tab_spec, tab_spec, tab_spec],
        out_specs=[pl.BlockSpec((Q_WIDTH, tm), col), pl.BlockSpec((tm, KV_WIDTH), row),
                   pl.BlockSpec((KV_WIDTH, tm), col), pl.BlockSpec((tm, CONV_WIDTH), row)],
        out_shape=[jax.ShapeDtypeStruct((Q_WIDTH, tt), BF16), jax.ShapeDtypeStruct((tt, KV_WIDTH), BF16),
                   jax.ShapeDtypeStruct((KV_WIDTH, tt), BF16), jax.ShapeDtypeStruct((tt, CONV_WIDTH), F32)],
        compiler_params=_params(),
        name="inproj_a",
    )(*x_parts, w, *tabs)


def _attn_kernel(sink_ref, q_ref, kp_ref, kc_ref, kn_ref, vp_ref, vc_ref, vn_ref, o_ref, ot_scr, *, dims):
    tp, sp, ss = dims
    off, slen = _seq_offset(pl.program_id(0) * BLOCK, tp, sp, ss)
    grp = N_HEADS // N_KV_HEADS
    no_prev = jnp.where(off > 0, 0, BLOCK)
    no_next = jnp.where(off + BLOCK < slen, 0, BLOCK)
    key = lax.broadcasted_iota(jnp.int32, (3 * BLOCK, grp * BLOCK), 0)
    qry = lax.broadcasted_iota(jnp.int32, (3 * BLOCK, grp * BLOCK), 1) % BLOCK
    jj = key % BLOCK
    blk = key // BLOCK
    valid = ((blk == 1) | ((blk == 0) & (jj >= qry + no_prev)) | ((blk == 2) & (jj + no_next <= qry)))
    kcat = jnp.concatenate([kp_ref[...], kc_ref[...], kn_ref[...]], axis=0)
    vtcat = jnp.concatenate([vp_ref[...], vc_ref[...], vn_ref[...]], axis=1)
    for h in range(N_KV_HEADS):
        heads = range(grp * h, grp * (h + 1))
        qw = jnp.concatenate([q_ref[HEAD_DIM * hh:HEAD_DIM * (hh + 1), :] for hh in heads], axis=1)
        sk = jnp.concatenate([jnp.full((1, BLOCK), sink_ref[hh], F32) for hh in heads], axis=1)
        s = _dot(kcat[:, HEAD_DIM * h:HEAD_DIM * (h + 1)], qw)
        s = jnp.where(valid, s, NEG_INF)
        m = jnp.maximum(jnp.max(s, axis=0, keepdims=True), sk)
        p = jnp.exp(s - m)
        denom = jnp.sum(p, axis=0, keepdims=True) + jnp.exp(sk - m)
        ot = _dot(vtcat[HEAD_DIM * h:HEAD_DIM * (h + 1), :], p.astype(BF16)) / denom
        for g, hh in enumerate(heads):
            ot_scr[HEAD_DIM * hh:HEAD_DIM * (hh + 1), :] = ot[:, BLOCK * g:BLOCK * (g + 1)]
    o_ref[...] = ot_scr[...].T.astype(BF16)


def _attention(qt, k, vt, sink, dims):
    tt = k.shape[0]
    nb = tt // BLOCK
    cur = lambda i: (i, 0)
    prv = lambda i: (jnp.maximum(i - 1, 0), 0)
    nxt = lambda i: (jnp.minimum(i + 1, nb - 1), 0)
    tr = lambda m: (lambda i: m(i)[::-1])
    kb = lambda m: pl.BlockSpec((BLOCK, KV_WIDTH), m)
    vb = lambda m: pl.BlockSpec((KV_WIDTH, BLOCK), tr(m))
    return pl.pallas_call(
        functools.partial(_attn_kernel, dims=dims),
        grid=(nb,),
        in_specs=[pl.BlockSpec(memory_space=pltpu.SMEM), pl.BlockSpec((Q_WIDTH, BLOCK), tr(cur)),
                  kb(prv), kb(cur), kb(nxt), vb(prv), vb(cur), vb(nxt)],
        out_specs=pl.BlockSpec((BLOCK, Q_WIDTH), cur),
        out_shape=jax.ShapeDtypeStruct((tt, Q_WIDTH), BF16),
        scratch_shapes=[pltpu.VMEM((Q_WIDTH, BLOCK), F32)],
        compiler_params=_params(),
        name="band_attention",
    )(sink.astype(F32), qt, k, k, k, vt, vt, vt)


def _conv_a_kernel(z_ref, zp_ref, zn_ref, w_ref, b_ref, g_ref, beta_ref, o_ref, ext_ref, *, dims, ts):
    tp, sp, ss = dims
    off, slen = _seq_offset(pl.program_id(0) * ts, tp, sp, ss)
    keep_prev = (off > 0).astype(F32)
    keep_next = (off + ts < slen).astype(F32)
    ext_ref[0:CONV_HALO, :] = zp_ref[...] * keep_prev
    ext_ref[CONV_HALO:CONV_HALO + ts, :] = z_ref[...]
    ext_ref[CONV_HALO + ts:, :] = zn_ref[...] * keep_next
    rows, lw = 32, 256
    n_a = -(-CONV_KSIZE // SUBLANES)
    for rc in range(ts // rows):
        r0 = rows * rc
        accs = []
        for lc in range(CONV_WIDTH // lw):
            ls = slice(lw * lc, lw * (lc + 1))
            acc = jnp.zeros((rows, lw), F32)
            for b in range(SUBLANES):
                part = jnp.zeros((rows + SUBLANES, lw), F32)
                for a in range(n_a):
                    k = SUBLANES * a + b
                    if k < CONV_KSIZE:
                        lo = r0 + SUBLANES * a
                        part = part + ext_ref[lo:lo + rows + SUBLANES, ls] * w_ref[k:k + 1, ls]
                acc = acc + part[b + 1:b + 1 + rows, :]
            accs.append(acc + b_ref[:, ls])
        y = _layer_norm(jnp.concatenate(accs, axis=1), g_ref[...], beta_ref[...])
        o_ref[r0:r0 + rows, :] = (y * jax.nn.sigmoid(y)).astype(BF16)


def _conv_a(z, w, b, g, beta, dims):
    tt, cw = z.shape
    tp, sp, ss = dims
    ts = min(256, sp, ss)
    hb = ts // CONV_HALO
    nh = tt // CONV_HALO
    vec = lambda a: a.reshape(1, cw)
    return pl.pallas_call(
        functools.partial(_conv_a_kernel, dims=dims, ts=ts),
        grid=(tt // ts,),
        in_specs=[pl.BlockSpec((ts, cw), lambda i: (i, 0)),
                  pl.BlockSpec((CONV_HALO, cw), lambda i: (jnp.maximum(i * hb - 1, 0), 0)),
                  pl.BlockSpec((CONV_HALO, cw), lambda i: (jnp.minimum((i + 1) * hb, nh - 1), 0)),
                  _resident((CONV_KSIZE, cw)), _resident((1, cw)), _resident((1, cw)), _resident((1, cw))],
        out_specs=pl.BlockSpec((ts, cw), lambda i: (i, 0)),
        out_shape=jax.ShapeDtypeStruct((tt, cw), BF16),
        scratch_shapes=[pltpu.VMEM((ts + 2 * CONV_HALO, cw), F32)],
        compiler_params=_params(),
        name="conformer_conv",
    )(z, z, z, w, vec(b), vec(g), vec(beta))


def _norm_and_route(mix, x, g_ref, b_ref, wr_ref, br_ref, outs, alpha):
    h_ref, hp_ref, eid_ref, wt_ref = outs
    h = _layer_norm(alpha * x + mix, g_ref[...], b_ref[...])
    h_ref[...] = h
    _store_token_tiles(hp_ref, _pack_rows(h))
    logits = _dot(h.astype(BF16), wr_ref[...]) + br_ref[...]
    lane = lax.broadcasted_iota(jnp.int32, logits.shape, 1).astype(F32)
    big = jnp.float32(LANES)
    ninf = jnp.float32(-jnp.inf)
    gmask = lane < N_GROUPS
    lg = jnp.where(gmask, logits, ninf)
    gmax = jnp.max(lg, axis=-1, keepdims=True)
    gidx = jnp.min(jnp.where(lg == gmax, lane, big), axis=-1, keepdims=True)
    gsum = jnp.sum(jnp.where(gmask, jnp.exp(logits - gmax), 0.0), axis=-1, keepdims=True)
    lo = N_GROUPS + EXPERTS_PER_GROUP * gidx
    le = jnp.where((lane >= lo) & (lane < lo + EXPERTS_PER_GROUP), logits, ninf)
    t0 = jnp.max(le, axis=-1, keepdims=True)
    i0 = jnp.min(jnp.where(le == t0, lane, big), axis=-1, keepdims=True)
    le1 = jnp.where(lane == i0, ninf, le)
    t1 = jnp.max(le1, axis=-1, keepdims=True)
    i1 = jnp.min(jnp.where(le1 == t1, lane, big), axis=-1, keepdims=True)
    r = jnp.exp(t1 - t0)
    w0 = 1.0 / ((1.0 + r) * gsum)
    w1 = r * w0
    eid_ref[...] = jnp.where(lane == 0, i0 - N_GROUPS, jnp.where(lane == 1, i1 - N_GROUPS, 0.0)).astype(jnp.int32)
    wt_ref[...] = jnp.where(lane == 0, w0, jnp.where(lane == 1, w1, 0.0))


def _outproj_a_kernel(*refs, alpha, n_x, nb1):
    att_ref, cv_ref = refs[:2]
    x_refs = refs[2:2 + n_x]
    w_ref, g_ref, b_ref, wr_ref, br_ref = refs[2 + n_x:7 + n_x]
    mix = _dot(att_ref[...], w_ref[0:Q_WIDTH, :]) + _dot(cv_ref[...], w_ref[Q_WIDTH:, :])
    _norm_and_route(mix, _read_rows(x_refs, nb1), g_ref, b_ref, wr_ref, br_ref, refs[7 + n_x:], alpha)


def _route_outs(tt, d, tm, lag=0):
    nb = tt // tm
    row = lambda i: (jnp.clip(i - lag, 0, nb - 1), 0)
    specs = [pl.BlockSpec((tm, d), row), pl.BlockSpec((tm * SUBLANES, LANES), row),
             pl.BlockSpec((tm, LANES), row), pl.BlockSpec((tm, LANES), row)]
    shapes = [jax.ShapeDtypeStruct((tt, d), F32), jax.ShapeDtypeStruct((tt * SUBLANES, LANES), jnp.uint32),
              jax.ShapeDtypeStruct((tt, LANES), jnp.int32), jax.ShapeDtypeStruct((tt, LANES), F32)]
    return specs, shapes


def _outproj_a(att, cv, x_parts, w, g, b, wr, br, alpha):
    tt = att.shape[0]
    d = w.shape[1]
    tm = min(512, min(a.shape[0] for a in x_parts))
    row = lambda i: (i, 0)
    out_specs, out_shape = _route_outs(tt, d, tm)
    return pl.pallas_call(
        functools.partial(_outproj_a_kernel, alpha=alpha, n_x=len(x_parts), nb1=_nb1(x_parts, tm)),
        grid=(tt // tm,),
        in_specs=[pl.BlockSpec((tm, Q_WIDTH), row), pl.BlockSpec((tm, CONV_WIDTH), row)]
        + _row_specs(x_parts, tm, d)
        + [_resident(w.shape), _resident((1, d)), _resident((1, d)), _resident(wr.shape), _resident((1, LANES))],
        out_specs=out_specs, out_shape=out_shape,
        compiler_params=_params(),
        name="outproj_a",
    )(att, cv, *x_parts, w, g.reshape(1, d), b.reshape(1, d), wr, br)


def _inproj_c_kernel(x_ref, w_ref, bg_ref, cx_ref):
    xb = x_ref[...].astype(BF16)
    d = x_ref.shape[1]
    cw = 512
    for c in range(d // cw):
        cs = slice(cw * c, cw * (c + 1))
        bg_ref[:, cs] = _dot(xb, w_ref[:, cw * c:cw * (c + 1)])
        cg = _dot(xb, w_ref[:, d + cw * c:d + cw * (c + 1)])
        xv = _dot(xb, w_ref[:, 2 * d + cw * c:2 * d + cw * (c + 1)])
        cx_ref[:, cs] = cg * xv


def _inproj_c(x, w):
    tt, d = x.shape
    tm = min(256, tt)
    row = lambda i: (i, 0)
    return pl.pallas_call(
        _inproj_c_kernel,
        grid=(tt // tm,),
        in_specs=[pl.BlockSpec((tm, d), row), _resident(w.shape)],
        out_specs=[pl.BlockSpec((tm, d), row), pl.BlockSpec((tm, d), row)],
        out_shape=[jax.ShapeDtypeStruct((tt, d), F32), jax.ShapeDtypeStruct((tt, d), F32)],
        compiler_params=_params(),
        name="inproj_c",
    )(x, w)


def _outproj_c_kernel(bg_ref, cx_ref, cp_ref, cn_ref, cw_ref, x_ref, w_ref, g_ref, b_ref, wr_ref, br_ref,
                      *outs, alpha, dims, tm):
    tp, sp, ss = dims
    off, slen = _seq_offset(pl.program_id(0) * tm, tp, sp, ss)
    keep_prev = (off > 0).astype(F32)
    keep_next = (off + tm < slen).astype(F32)
    cx = cx_ref[...]
    first = cp_ref[SHORT_HALO - 1:SHORT_HALO, :] * keep_prev
    last = cn_ref[0:1, :] * keep_next
    rid = lax.broadcasted_iota(jnp.int32, cx.shape, 0)
    before = jnp.where(rid == 0, first, pltpu.roll(cx, 1, 0))
    after = jnp.where(rid == tm - 1, last, pltpu.roll(cx, tm - 1, 0))
    conv = before * cw_ref[0:1, :] + cx * cw_ref[1:2, :] + after * cw_ref[2:3, :]
    y = (bg_ref[...] * conv).astype(BF16)
    _norm_and_route(_dot(y, w_ref[...]), x_ref[...], g_ref, b_ref, wr_ref, br_ref, outs, alpha)


def _outproj_c(bg, cx, cw, x, w, g, b, wr, br, alpha, dims):
    tt, d = x.shape
    tp, sp, ss = dims
    tm = min(512, sp, ss)
    hb = tm // SHORT_HALO
    nh = tt // SHORT_HALO
    row = lambda i: (i, 0)
    out_specs, out_shape = _route_outs(tt, d, tm)
    return pl.pallas_call(
        functools.partial(_outproj_c_kernel, alpha=alpha, dims=dims, tm=tm),
        grid=(tt // tm,),
        in_specs=[pl.BlockSpec((tm, d), row), pl.BlockSpec((tm, d), row),
                  pl.BlockSpec((SHORT_HALO, d), lambda i: (jnp.maximum(i * hb - 1, 0), 0)),
                  pl.BlockSpec((SHORT_HALO, d), lambda i: (jnp.minimum((i + 1) * hb, nh - 1), 0)),
                  _resident(cw.shape), pl.BlockSpec((tm, d), row), _resident(w.shape),
                  _resident((1, d)), _resident((1, d)), _resident(wr.shape), _resident((1, LANES))],
        out_specs=out_specs, out_shape=out_shape,
        compiler_params=_params(),
        name="outproj_c",
    )(bg, cx, cx, cx, cw, x, w, g.reshape(1, d), b.reshape(1, d), wr, br)


def _plan_kernel(eid_ref, pos_ref, tmeta_ref, pbase_ref, cbase_ref, *, tile, n_tiles_pad, n_codes):
    phase = pl.program_id(0)
    t = pl.program_id(1)
    tp = eid_ref.shape[0]
    lane = lax.broadcasted_iota(jnp.int32, (tp, LANES), 1)
    oh0 = lane == eid_ref[:, 0:1]
    oh1 = lane == eid_ref[:, 1:2]
    chosen = jnp.where(oh0 | oh1, 1.0, 0.0)
    counts = jnp.sum(chosen, axis=0, keepdims=True)

    @pl.when((phase == 0) & (t == 0))
    def _():
        pbase_ref[...] = jnp.zeros_like(pbase_ref)

    @pl.when(phase == 0)
    def _():
        pbase_ref[...] += counts

    @pl.when((phase == 1) & (t == 0))
    def _():
        totals = pbase_ref[...]
        ntile = jnp.floor((totals + (tile - 1)) * (1.0 / tile))
        hi = jnp.floor(totals * (1.0 / 256.0))
        lo = totals - 256.0 * hi
        r = lax.broadcasted_iota(jnp.int32, (LANES, LANES), 0)
        c = lax.broadcasted_iota(jnp.int32, (LANES, LANES), 1)
        upper = jnp.where(r < c, 1.0, 0.0).astype(BF16)

        def prefix(v):
            return _dot(jnp.broadcast_to(v, (SUBLANES, LANES)).astype(BF16), upper)[0:1, :]

        tstart = prefix(ntile)
        cstart = 256.0 * prefix(hi) + prefix(lo)
        pbase_ref[...] = tstart * tile
        cbase_ref[...] = cstart
        j = lax.broadcasted_iota(jnp.int32, (n_tiles_pad, LANES), 0).astype(F32)
        ln = lax.broadcasted_iota(jnp.int32, (n_tiles_pad, LANES), 1)
        done = jnp.where((tstart + ntile <= j) & (ln < N_EXPERTS), 1.0, 0.0)
        texp = jnp.minimum(jnp.sum(done, axis=-1, keepdims=True), N_EXPERTS - 1.0)
        total = jnp.sum(jnp.where(ln[0:1, :] < N_EXPERTS, ntile, 0.0), axis=-1, keepdims=True)
        mine = ln.astype(F32) == texp
        pick = lambda v: jnp.sum(jnp.where(mine, v, 0.0), axis=-1, keepdims=True)
        first = pick(cstart) + (j[:, 0:1] - pick(tstart)) * tile
        first = jnp.clip(first, 0.0, n_codes - 1.0)
        seg_end = pick(tstart + ntile)
        seg_ord = pick(prefix(jnp.minimum(ntile, 1.0)))
        tmeta_ref[...] = jnp.where(ln == 1, total, jnp.where(ln == 2, first, jnp.where(
            ln == 3, seg_end, jnp.where(ln == 4, seg_ord, texp)))).astype(jnp.int32)

    @pl.when(phase == 1)
    def _():
        r = lax.broadcasted_iota(jnp.int32, (tp, tp), 0)
        c = lax.broadcasted_iota(jnp.int32, (tp, tp), 1)
        lower = jnp.where(c < r, 1.0, 0.0).astype(BF16)
        rank = _dot(lower, chosen.astype(BF16))
        pslot = pbase_ref[...] + rank
        cslot = cbase_ref[...] + rank
        pick = lambda oh, v: jnp.sum(jnp.where(oh, v, 0.0), axis=-1, keepdims=True)
        out = jnp.where(lane == 0, pick(oh0, pslot), jnp.where(lane == 1, pick(oh1, pslot),
              jnp.where(lane == 2, pick(oh0, cslot), jnp.where(lane == 3, pick(oh1, cslot), 0.0))))
        pos_ref[...] = out.astype(jnp.int32)
        pbase_ref[...] += counts
        cbase_ref[...] += counts


def _plan(eid, n_tiles_pad):
    tt = eid.shape[0]
    tp = min(512, tt)
    return pl.pallas_call(
        functools.partial(_plan_kernel, tile=MOE_TILE, n_tiles_pad=n_tiles_pad, n_codes=2 * tt),
        grid=(2, tt // tp),
        in_specs=[pl.BlockSpec((tp, LANES), lambda ph, t: (t, 0))],
        out_specs=[pl.BlockSpec((tp, LANES), lambda ph, t: (t * ph, 0)),
                   pl.BlockSpec((n_tiles_pad, LANES), lambda ph, t: (0, 0))],
        out_shape=[jax.ShapeDtypeStruct((tt, LANES), jnp.int32),
                   jax.ShapeDtypeStruct((n_tiles_pad, LANES), jnp.int32)],
        scratch_shapes=[pltpu.VMEM((1, LANES), F32), pltpu.VMEM((1, LANES), F32)],
        compiler_params=_params(2),
        name="moe_plan",
    )(eid)


def _invert_kernel(cpos_ref, code_ref, *, tt, tok):
    step = pl.program_id(0)

    @pl.when(step == 0)
    def _():
        for r in range(MOE_TILE):
            code_ref[2 * tt + r] = 0

    def put(r, _):
        t = step * tok + r
        code_ref[cpos_ref[t]] = SUBLANES * t
        code_ref[cpos_ref[tt + t]] = SUBLANES * t
        return 0

    lax.fori_loop(0, tok, put, 0, unroll=DMA_UNROLL)


def _invert(cpos_flat, tt):
    tok = min(512, tt)
    grid_spec = pltpu.PrefetchScalarGridSpec(
        num_scalar_prefetch=1, grid=(tt // tok,), in_specs=[],
        out_specs=pl.BlockSpec(memory_space=pltpu.SMEM))
    return pl.pallas_call(
        functools.partial(_invert_kernel, tt=tt, tok=tok),
        grid_spec=grid_spec,
        out_shape=jax.ShapeDtypeStruct((2 * tt + MOE_TILE,), jnp.int32),
        compiler_params=_params(),
        name="moe_invert",
    )(cpos_flat)


def _moe_kernel(texp_ref, nused_ref, first_ref, segend_ref, segord_ref, code_ref, hp_ref, wup_ref, wdn_ref,
                o_ref, xbuf, gsem, wup_f, wdn_f, wsem, wup_b, wdn_b, *, tm, layer):
    j = pl.program_id(0)
    nused = nused_ref[0]
    slot = j % 2

    def weight_copies(expert, par):
        e = layer * N_EXPERTS + expert
        return (pltpu.make_async_copy(wup_ref.at[e], wup_f.at[par], wsem.at[par]),
                pltpu.make_async_copy(wdn_ref.at[e], wdn_f.at[par], wsem.at[par]))

    def gather(tile, buf, rows):
        first = first_ref[tile]
        for r in rows:
            src = pl.multiple_of(code_ref[first + r], SUBLANES)
            pltpu.make_async_copy(hp_ref.at[pl.ds(src, SUBLANES)],
                                  xbuf.at[buf, pl.ds(SUBLANES * r, SUBLANES)], gsem.at[buf]).start()

    def wait_tile(buf):
        pltpu.make_async_copy(hp_ref.at[pl.ds(0, SUBLANES * tm)], xbuf.at[buf], gsem.at[buf]).wait()

    @pl.when(j == 0)
    def _():
        for c in weight_copies(texp_ref[0], 0):
            c.start()
        gather(0, 0, range(tm))

    @pl.when(j < nused)
    def _():
        gather(j + 1, 1 - slot, range(tm))

    @pl.when((j < nused) & ((j == 0) | (texp_ref[j] != texp_ref[jnp.maximum(j - 1, 0)])))
    def _():
        par = segord_ref[j] % 2
        for c in weight_copies(texp_ref[j], par):
            c.wait()
        wup_b[...] = wup_f[par].astype(BF16)
        wdn_b[...] = wdn_f[par].astype(BF16)
        nxt = segend_ref[j]

        @pl.when(nxt < nused)
        def _():
            for c in weight_copies(texp_ref[nxt], 1 - par):
                c.start()

    @pl.when(j < nused)
    def _():
        wait_tile(slot)
        xb = _unpack_rows(_load_token_tiles(xbuf.at[slot], tm)).astype(BF16)
        up = _dot(xb, wup_b[...])
        a = up[:, :D_EXPERT]
        hid = (a * jax.nn.sigmoid(a) * up[:, D_EXPERT:]).astype(BF16)
        _store_token_tiles(o_ref, _pack_rows(_dot(hid, wdn_b[...])))

    @pl.when(j == nused)
    def _():
        wait_tile(slot)

    @pl.when(j >= nused)
    def _():
        o_ref[...] = jnp.zeros_like(o_ref)


def _moe(tmeta, code, hp, wup, wdn, n_tiles, layer):
    tm = MOE_TILE
    any_spec = pl.BlockSpec(memory_space=pl.ANY)
    grid_spec = pltpu.PrefetchScalarGridSpec(
        num_scalar_prefetch=6,
        grid=(n_tiles + 1,),
        in_specs=[any_spec, any_spec, any_spec],
        out_specs=pl.BlockSpec((tm * SUBLANES, LANES), lambda j, *_: (j, 0)),
        scratch_shapes=[pltpu.VMEM((2, tm * SUBLANES, LANES), jnp.uint32), pltpu.SemaphoreType.DMA((2,)),
                        pltpu.VMEM((2,) + wup.shape[1:], F32), pltpu.VMEM((2,) + wdn.shape[1:], F32),
                        pltpu.SemaphoreType.DMA((2,)),
                        pltpu.VMEM(wup.shape[1:], BF16), pltpu.VMEM(wdn.shape[1:], BF16)],
    )
    return pl.pallas_call(
        functools.partial(_moe_kernel, tm=tm, layer=layer),
        grid_spec=grid_spec,
        out_shape=jax.ShapeDtypeStruct(((n_tiles + 1) * tm * SUBLANES, LANES), jnp.uint32),
        compiler_params=_params(),
        name="moe_experts",
    )(tmeta[:, 0], tmeta[0:1, 1], tmeta[:, 2], tmeta[:, 3], tmeta[:, 4], code, hp, wup, wdn)


def _combine_kernel(pos_ref, *refs, alpha, tm, tt, n_p, nb1, n_out):
    h_ref, ys_ref, wt_ref = refs[:3]
    p_refs = refs[3:3 + n_p]
    g_ref, b_ref, wg_ref, bg_ref, wp_ref = refs[3 + n_p:8 + n_p]
    o_refs = refs[8 + n_p:8 + n_p + n_out]
    ybuf, sem = refs[8 + n_p + n_out:]
    i = pl.program_id(0)
    slot = i % 2

    last = pl.num_programs(0) - 1

    def gather(step, buf, k, rows):
        base = k * tt + step * tm
        for r in rows:
            src = pl.multiple_of(pos_ref[base + r], SUBLANES)
            pltpu.make_async_copy(ys_ref.at[pl.ds(src, SUBLANES)],
                                  ybuf.at[buf, k, pl.ds(SUBLANES * r, SUBLANES)], sem.at[buf]).start()

    def wait_step(buf):
        for k in range(2):
            pltpu.make_async_copy(ys_ref.at[pl.ds(0, SUBLANES * tm)], ybuf.at[buf, k], sem.at[buf]).wait()

    @pl.when(i == 0)
    def _():
        for k in range(2):
            gather(0, 0, k, range(tm))

    @pl.when(i < last)
    def _():
        for k in range(2):
            gather(i + 1, 1 - slot, k, range(tm))

    wait_step(slot)
    wt = wt_ref[...]
    y0 = _unpack_rows(_load_token_tiles(ybuf.at[slot, 0], tm))
    y1 = _unpack_rows(_load_token_tiles(ybuf.at[slot, 1], tm))
    moe = y0 * wt[:, 0:1] + y1 * wt[:, 1:2]
    h2 = _layer_norm(alpha * h_ref[...] + moe, g_ref[...], b_ref[...])
    gate = jax.nn.sigmoid(_dot(h2.astype(BF16), wg_ref[...]) + bg_ref[...])
    x_new = h2 + gate * _dot(_read_rows(p_refs, nb1).astype(BF16), wp_ref[...])

    if n_out == 1:
        o_refs[0][...] = x_new
    else:
        @pl.when(i < nb1)
        def _():
            o_refs[0][...] = x_new

        @pl.when(i >= nb1)
        def _():
            o_refs[1][...] = x_new


def _combine(pos_flat, h, ys, wt, p_parts, g, b, wg, bg, wp, alpha, split_rows):
    tt, d = h.shape
    tm = min(256, min(a.shape[0] for a in p_parts))
    row = lambda i, *_: (i, 0)
    nb1 = _nb1(p_parts, tm)
    if split_rows is None:
        out_specs = pl.BlockSpec((tm, d), row)
        out_shape = jax.ShapeDtypeStruct((tt, d), F32)
        n_out = 1
    else:
        assert split_rows == p_parts[0].shape[0]
        out_specs = [pl.BlockSpec((tm, d), lambda i, *_: (jnp.minimum(i, nb1 - 1), 0)),
                     pl.BlockSpec((tm, d), lambda i, *_: (jnp.maximum(i - nb1, 0), 0))]
        out_shape = [jax.ShapeDtypeStruct((split_rows, d), F32), jax.ShapeDtypeStruct((tt - split_rows, d), F32)]
        n_out = 2
    grid_spec = pltpu.PrefetchScalarGridSpec(
        num_scalar_prefetch=1,
        grid=(tt // tm,),
        in_specs=[pl.BlockSpec((tm, d), row), pl.BlockSpec(memory_space=pl.ANY), pl.BlockSpec((tm, LANES), row)]
        + _row_specs(p_parts, tm, p_parts[0].shape[1])
        + [_resident((1, d)), _resident((1, d)), _resident(wg.shape), _resident((1, d)), _resident(wp.shape)],
        out_specs=out_specs,
        scratch_shapes=[pltpu.VMEM((2, 2, tm * SUBLANES, LANES), jnp.uint32), pltpu.SemaphoreType.DMA((2,))],
    )
    return pl.pallas_call(
        functools.partial(_combine_kernel, alpha=alpha, tm=tm, tt=tt, n_p=len(p_parts), nb1=nb1, n_out=n_out),
        grid_spec=grid_spec,
        out_shape=out_shape,
        compiler_params=_params(),
        name="moe_combine_ple",
    )(pos_flat, h, ys, wt, *p_parts, g.reshape(1, d), b.reshape(1, d), wg, bg.reshape(1, d), wp)


def _moe_block(hp, eid, wup, wdn, layer):
    tt = eid.shape[0]
    n_tiles = (2 * tt + N_EXPERTS * (MOE_TILE - 1)) // MOE_TILE
    n_tiles_pad = -(-(n_tiles + 1) // SUBLANES) * SUBLANES
    pos, tmeta = _plan(eid, n_tiles_pad)
    code = _invert(jnp.concatenate([pos[:, 2], pos[:, 3]]), tt)
    ys = _moe(tmeta, code, hp, wup, wdn, n_tiles, layer)
    return ys, SUBLANES * jnp.concatenate([pos[:, 0], pos[:, 1]])


def kernel(x_prompt, x_sample, p_prompt, p_sample, w_in_a, sink, conv_dw_a, conv_b_a, conv_ln_g, conv_ln_b,
           w_out_a, w_in_c, conv_w_c, w_out_c, ln1_g, ln1_b, ln2_g, ln2_b, w_router_group, b_router_group,
           w_router_expert, b_router_expert, w_up, w_down, w_ple, w_ple_gate, b_ple_gate):
    bp, sp, d = x_prompt.shape
    bs, ss, _ = x_sample.shape
    depth = p_prompt.shape[0]
    tp = bp * sp
    ts = bs * ss
    dims = (tp, sp, ss)
    alpha = (2.0 * depth) ** 0.25

    x_parts = (x_prompt.reshape(tp, d), x_sample.reshape(ts, d))
    tabs = _rope_tables(max(sp, ss))
    pad = LANES - N_GROUPS - N_EXPERTS
    w_route = jnp.concatenate(
        [w_router_group, w_router_expert, jnp.zeros((depth, d, pad), F32)], axis=-1).astype(BF16)
    b_route = jnp.concatenate(
        [b_router_group, b_router_expert, jnp.zeros((depth, pad), F32)], axis=-1).reshape(depth, 1, LANES)

    w_up_all = w_up.reshape((depth * N_EXPERTS,) + w_up.shape[2:])
    w_down_all = w_down.reshape((depth * N_EXPERTS,) + w_down.shape[2:])

    for i in range(depth):
        j = i // 2
        wr, br = w_route[i], b_route[i]
        if i % 2 == 0:
            q, k, v, z = _inproj_a(x_parts, w_in_a[j].astype(BF16), tabs, dims)
            att = _attention(q, k, v, sink[j], dims)
            cv = _conv_a(z, conv_dw_a[j], conv_b_a[j], conv_ln_g[j], conv_ln_b[j], dims)
            h, hp, eid, wt = _outproj_a(att, cv, x_parts, w_out_a[j].astype(BF16), ln1_g[i], ln1_b[i], wr, br, alpha)
        else:
            bg, cx = _inproj_c(x_parts[0], w_in_c[j].astype(BF16))
            h, hp, eid, wt = _outproj_c(bg, cx, conv_w_c[j], x_parts[0], w_out_c[j].astype(BF16), ln1_g[i], ln1_b[i],
                                    wr, br, alpha, dims)
        ys, pos_flat = _moe_block(hp, eid, w_up_all, w_down_all, i)
        p_parts = (p_prompt[i].reshape(tp, -1), p_sample[i].reshape(ts, -1))
        out = _combine(pos_flat, h, ys, wt, p_parts, ln2_g[i], ln2_b[i], w_ple_gate[i].astype(BF16),
                       b_ple_gate[i], w_ple[i].astype(BF16), alpha, tp if i == depth - 1 else None)
        x_parts = (out,)

    y_prompt, y_sample = out
    return y_prompt.reshape(bp, sp, d), y_sample.reshape(bs, ss, d)
```

```python
import functools

import jax
import jax.numpy as jnp
from jax import lax
from jax.experimental import pallas as pl
from jax.experimental.pallas import tpu as pltpu

F32 = jnp.float32
BF16 = jnp.bfloat16

N_HEADS = 16
N_KV_HEADS = 4
HEAD_DIM = 64
Q_WIDTH = N_HEADS * HEAD_DIM
KV_WIDTH = N_KV_HEADS * HEAD_DIM
BLOCK = 128
ROT_DIM = HEAD_DIM // 4
ROPE_THETA = 500000.0
CONV_WIDTH = 1024
CONV_KSIZE = 31
CONV_HALO = 16
SHORT_HALO = 8
N_GROUPS = 4
EXPERTS_PER_GROUP = 8
N_EXPERTS = N_GROUPS * EXPERTS_PER_GROUP
D_EXPERT = 512
LN_EPS = 1e-5
NEG_INF = -1e30
LANES = 128
SUBLANES = 8

MOE_TILE = 256
DMA_UNROLL = 8
VMEM_LIMIT = 56 * 1024 * 1024


def _params(n_axes=1):
    return pltpu.CompilerParams(dimension_semantics=("arbitrary",) * n_axes,
                                vmem_limit_bytes=VMEM_LIMIT)


def _resident(shape):
    nd = len(shape)
    return pl.BlockSpec(shape, lambda *_: (0,) * nd, pipeline_mode=pl.Buffered(1))


def _dot(a, b):
    return jnp.dot(a, b, preferred_element_type=F32)


def _layer_norm(y, g, b):
    mu = jnp.mean(y, axis=-1, keepdims=True)
    d = y - mu
    var = jnp.mean(d * d, axis=-1, keepdims=True)
    return d * lax.rsqrt(var + LN_EPS) * g + b


def _seq_offset(r0, tp, sp, ss):
    in_p = r0 < tp
    off = jnp.where(in_p, r0 % sp, (r0 - tp) % ss)
    return off, jnp.where(in_p, sp, ss)


def _row_specs(parts, tm, width, lag=0):
    nb = sum(a.shape[0] for a in parts) // tm
    tile = lambda i: jnp.clip(i - lag, 0, nb - 1)
    if len(parts) == 1:
        return [pl.BlockSpec((tm, width), lambda i, *_: (tile(i), 0))]
    nb1 = parts[0].shape[0] // tm
    return [pl.BlockSpec((tm, width), lambda i, *_: (jnp.minimum(tile(i), nb1 - 1), 0)),
            pl.BlockSpec((tm, width), lambda i, *_: (jnp.maximum(tile(i) - nb1, 0), 0))]


def _read_rows(refs, nb1, lag=0):
    if len(refs) == 1:
        return refs[0][...]
    return jnp.where(pl.program_id(0) - lag < nb1, refs[0][...], refs[1][...])


def _nb1(parts, tm):
    return parts[0].shape[0] // tm


def _pack_rows(x):
    n = x.shape[1] // 2
    lo = lax.bitcast_convert_type(x[:, :n].astype(BF16).astype(F32), jnp.uint32) >> 16
    hi = lax.bitcast_convert_type(x[:, n:].astype(BF16).astype(F32), jnp.uint32) & jnp.uint32(0xFFFF0000)
    return lo | hi


def _unpack_rows(w):
    lo = lax.bitcast_convert_type(w << 16, F32)
    hi = lax.bitcast_convert_type(w & jnp.uint32(0xFFFF0000), F32)
    return jnp.concatenate([lo, hi], axis=1)


def _store_token_tiles(ref, w):
    m = w.shape[0]
    for s in range(SUBLANES):
        ref[pl.ds(s, m, stride=SUBLANES), :] = w[:, LANES * s:LANES * (s + 1)]


def _load_token_tiles(ref, m):
    return jnp.concatenate([ref[pl.ds(s, m, stride=SUBLANES), :] for s in range(SUBLANES)], axis=1)


def _inproj_a_kernel(*refs, n_x, nb1):
    x_refs, (w_ref, rc_ref, ra_ref, rb_ref, q_ref, k_ref, v_ref, z_ref) = refs[:n_x], refs[n_x:]
    xb = _read_rows(x_refs, nb1).astype(BF16)
    rc, ra, rb = rc_ref[...], ra_ref[...], rb_ref[...]

    def rope(u):
        outs = []
        half = ROT_DIM // 2
        for j in range(u.shape[1] // LANES):
            t = u[:, LANES * j:LANES * (j + 1)]
            outs.append(t * rc + pltpu.roll(t, LANES - half, 1) * ra + pltpu.roll(t, half, 1) * rb)
        return jnp.concatenate(outs, axis=1)

    cw = 512
    for c in range(Q_WIDTH // cw):
        u = _dot(xb, w_ref[:, cw * c:cw * (c + 1)])
        q_ref[cw * c:cw * (c + 1), :] = (rope(u) * (HEAD_DIM ** -0.5)).T.astype(BF16)
    u = _dot(xb, w_ref[:, Q_WIDTH:Q_WIDTH + 2 * KV_WIDTH])
    k_ref[...] = rope(u[:, :KV_WIDTH]).astype(BF16)
    v_ref[...] = u[:, KV_WIDTH:].T.astype(BF16)
    base = Q_WIDTH + 2 * KV_WIDTH
    for c in range(CONV_WIDTH // cw):
        a = _dot(xb, w_ref[:, base + cw * c:base + cw * (c + 1)])
        g = _dot(xb, w_ref[:, base + CONV_WIDTH + cw * c:base + CONV_WIDTH + cw * (c + 1)])
        z_ref[:, cw * c:cw * (c + 1)] = a * jax.nn.sigmoid(g)


def _rope_tables(smax):
    half = ROT_DIM // 2
    inv_freq = ROPE_THETA ** (-(jnp.arange(half, dtype=F32) * 2.0 / ROT_DIM))
    ang = jnp.arange(smax, dtype=F32)[:, None] * inv_freq[None, :]
    cos, sin = jnp.cos(ang), jnp.sin(ang)
    d = jnp.arange(LANES) % HEAD_DIM
    f = d % half
    rc = jnp.where(d[None, :] < ROT_DIM, cos[:, f], 1.0)
    ra = jnp.where(d[None, :] < half, -sin[:, f], 0.0)
    rb = jnp.where((d[None, :] >= half) & (d[None, :] < ROT_DIM), sin[:, f], 0.0)
    return rc.astype(F32), ra.astype(F32), rb.astype(F32)


def _inproj_a(x_parts, w, tabs, dims):
    tp, sp, ss = dims
    tt = sum(a.shape[0] for a in x_parts)
    d = x_parts[0].shape[1]
    tm = min(512, sp, ss)
    n_in = w.shape[1]

    def tab_map(i):
        off, _ = _seq_offset(i * tm, tp, sp, ss)
        return (off // tm, 0)

    row = lambda i: (i, 0)
    col = lambda i: (0, i)
    tab_spec = pl.BlockSpec((tm, LANES), tab_map)
    return pl.pallas_call(
        functools.partial(_inproj_a_kernel, n_x=len(x_parts), nb1=_nb1(x_parts, tm)),
        grid=(tt // tm,),
        in_specs=_row_specs(x_parts, tm, d) + [_resident((d, n_in)), tab_spec, tab_spec, tab_spec],
        out_specs=[pl.BlockSpec((Q_WIDTH, tm), col), pl.BlockSpec((tm, KV_WIDTH), row),
                   pl.BlockSpec((KV_WIDTH, tm), col), pl.BlockSpec((tm, CONV_WIDTH), row)],
        out_shape=[jax.ShapeDtypeStruct((Q_WIDTH, tt), BF16), jax.ShapeDtypeStruct((tt, KV_WIDTH), BF16),
                   jax.ShapeDtypeStruct((KV_WIDTH, tt), BF16), jax.ShapeDtypeStruct((tt, CONV_WIDTH), F32)],
        compiler_params=_params(),
        name="inproj_a",
    )(*x_parts, w, *tabs)


def _attn_kernel(sink_ref, q_ref, kp_ref, kc_ref, kn_ref, vp_ref, vc_ref, vn_ref, o_ref, ot_scr, *, dims):
    tp, sp, ss = dims
    off, slen = _seq_offset(pl.program_id(0) * BLOCK, tp, sp, ss)
    grp = N_HEADS // N_KV_HEADS
    no_prev = jnp.where(off > 0, 0, BLOCK)
    no_next = jnp.where(off + BLOCK < slen, 0, BLOCK)
    key = lax.broadcasted_iota(jnp.int32, (3 * BLOCK, grp * BLOCK), 0)
    qry = lax.broadcasted_iota(jnp.int32, (3 * BLOCK, grp * BLOCK), 1) % BLOCK
    jj = key % BLOCK
    blk = key // BLOCK
    valid = ((blk == 1) | ((blk == 0) & (jj >= qry + no_prev)) | ((blk == 2) & (jj + no_next <= qry)))
    kcat = jnp.concatenate([kp_ref[...], kc_ref[...], kn_ref[...]], axis=0)
    vtcat = jnp.concatenate([vp_ref[...], vc_ref[...], vn_ref[...]], axis=1)
    for h in range(N_KV_HEADS):
        heads = range(grp * h, grp * (h + 1))
        qw = jnp.concatenate([q_ref[HEAD_DIM * hh:HEAD_DIM * (hh + 1), :] for hh in heads], axis=1)
        sk = jnp.concatenate([jnp.full((1, BLOCK), sink_ref[hh], F32) for hh in heads], axis=1)
        s = _dot(kcat[:, HEAD_DIM * h:HEAD_DIM * (h + 1)], qw)
        s = jnp.where(valid, s, NEG_INF)
        m = jnp.maximum(jnp.max(s, axis=0, keepdims=True), sk)
        p = jnp.exp(s - m)
        denom = jnp.sum(p, axis=0, keepdims=True) + jnp.exp(sk - m)
        ot = _dot(vtcat[HEAD_DIM * h:HEAD_DIM * (h + 1), :], p.astype(BF16)) / denom
        for g, hh in enumerate(heads):
            ot_scr[HEAD_DIM * hh:HEAD_DIM * (hh + 1), :] = ot[:, BLOCK * g:BLOCK * (g + 1)]
    o_ref[...] = ot_scr[...].T.astype(BF16)


def _attention(qt, k, vt, sink, dims):
    tt = k.shape[0]
    nb = tt // BLOCK
    cur = lambda i: (i, 0)
    prv = lambda i: (jnp.maximum(i - 1, 0), 0)
    nxt = lambda i: (jnp.minimum(i + 1, nb - 1), 0)
    tr = lambda m: (lambda i: m(i)[::-1])
    kb = lambda m: pl.BlockSpec((BLOCK, KV_WIDTH), m)
    vb = lambda m: pl.BlockSpec((KV_WIDTH, BLOCK), tr(m))
    return pl.pallas_call(
        functools.partial(_attn_kernel, dims=dims),
        grid=(nb,),
        in_specs=[pl.BlockSpec(memory_space=pltpu.SMEM), pl.BlockSpec((Q_WIDTH, BLOCK), tr(cur)),
                  kb(prv), kb(cur), kb(nxt), vb(prv), vb(cur), vb(nxt)],
        out_specs=pl.BlockSpec((BLOCK, Q_WIDTH), cur),
        out_shape=jax.ShapeDtypeStruct((tt, Q_WIDTH), BF16),
        scratch_shapes=[pltpu.VMEM((Q_WIDTH, BLOCK), F32)],
        compiler_params=_params(),
        name="band_attention",
    )(sink.astype(F32), qt, k, k, k, vt, vt, vt)


def _conv_a_kernel(z_ref, zp_ref, zn_ref, w_ref, b_ref, g_ref, beta_ref, o_ref, ext_ref, *, dims, ts):
    tp, sp, ss = dims
    off, slen = _seq_offset(pl.program_id(0) * ts, tp, sp, ss)
    keep_prev = (off > 0).astype(F32)
    keep_next = (off + ts < slen).astype(F32)
    ext_ref[0:CONV_HALO, :] = zp_ref[...] * keep_prev
    ext_ref[CONV_HALO:CONV_HALO + ts, :] = z_ref[...]
    ext_ref[CONV_HALO + ts:, :] = zn_ref[...] * keep_next
    rows, lw = 128, 128
    n_a = -(-CONV_KSIZE // SUBLANES)
    for rc in range(ts // rows):
        r0 = rows * rc
        accs = []
        for lc in range(CONV_WIDTH // lw):
            ls = slice(lw * lc, lw * (lc + 1))
            acc = jnp.zeros((rows, lw), F32)
            for b in range(SUBLANES):
                part = jnp.zeros((rows + SUBLANES, lw), F32)
                for a in range(n_a):
                    k = SUBLANES * a + b
                    if k < CONV_KSIZE:
                        lo = r0 + SUBLANES * a
                        part = part + ext_ref[lo:lo + rows + SUBLANES, ls] * w_ref[k:k + 1, ls]
                acc = acc + part[b + 1:b + 1 + rows, :]
            accs.append(acc + b_ref[:, ls])
        y = _layer_norm(jnp.concatenate(accs, axis=1), g_ref[...], beta_ref[...])
        o_ref[r0:r0 + rows, :] = (y * jax.nn.sigmoid(y)).astype(BF16)


def _conv_a(z, w, b, g, beta, dims):
    tt, cw = z.shape
    tp, sp, ss = dims
    ts = min(256, sp, ss)
    hb = ts // CONV_HALO
    nh = tt // CONV_HALO
    vec = lambda a: a.reshape(1, cw)
    return pl.pallas_call(
        functools.partial(_conv_a_kernel, dims=dims, ts=ts),
        grid=(tt // ts,),
        in_specs=[pl.BlockSpec((ts, cw), lambda i: (i, 0)),
                  pl.BlockSpec((CONV_HALO, cw), lambda i: (jnp.maximum(i * hb - 1, 0), 0)),
                  pl.BlockSpec((CONV_HALO, cw), lambda i: (jnp.minimum((i + 1) * hb, nh - 1), 0)),
                  _resident((CONV_KSIZE, cw)), _resident((1, cw)), _resident((1, cw)), _resident((1, cw))],
        out_specs=pl.BlockSpec((ts, cw), lambda i: (i, 0)),
        out_shape=jax.ShapeDtypeStruct((tt, cw), BF16),
        scratch_shapes=[pltpu.VMEM((ts + 2 * CONV_HALO, cw), F32)],
        compiler_params=_params(),
        name="conformer_conv",
    )(z, z, z, w, vec(b), vec(g), vec(beta))


def _norm_and_route(mix, x, g_ref, b_ref, wr_ref, br_ref, outs, alpha):
    h_ref, hp_ref, eid_ref, wt_ref = outs
    h = _layer_norm(alpha * x + mix, g_ref[...], b_ref[...])
    h_ref[...] = h
    _store_token_tiles(hp_ref, _pack_rows(h))
    logits = _dot(h.astype(BF16), wr_ref[...]) + br_ref[...]
    lane = lax.broadcasted_iota(jnp.int32, logits.shape, 1).astype(F32)
    big = jnp.float32(LANES)
    ninf = jnp.float32(-jnp.inf)
    gmask = lane < N_GROUPS
    lg = jnp.where(gmask, logits, ninf)
    gmax = jnp.max(lg, axis=-1, keepdims=True)
    gidx = jnp.min(jnp.where(lg == gmax, lane, big), axis=-1, keepdims=True)
    gsum = jnp.sum(jnp.where(gmask, jnp.exp(logits - gmax), 0.0), axis=-1, keepdims=True)
    lo = N_GROUPS + EXPERTS_PER_GROUP * gidx
    le = jnp.where((lane >= lo) & (lane < lo + EXPERTS_PER_GROUP), logits, ninf)
    t0 = jnp.max(le, axis=-1, keepdims=True)
    i0 = jnp.min(jnp.where(le == t0, lane, big), axis=-1, keepdims=True)
    le1 = jnp.where(lane == i0, ninf, le)
    t1 = jnp.max(le1, axis=-1, keepdims=True)
    i1 = jnp.min(jnp.where(le1 == t1, lane, big), axis=-1, keepdims=True)
    r = jnp.exp(t1 - t0)
    w0 = 1.0 / ((1.0 + r) * gsum)
    w1 = r * w0
    eid_ref[...] = jnp.where(lane == 0, i0 - N_GROUPS, jnp.where(lane == 1, i1 - N_GROUPS, 0.0)).astype(jnp.int32)
    wt_ref[...] = jnp.where(lane == 0, w0, jnp.where(lane == 1, w1, 0.0))


def _outproj_a_kernel(*refs, alpha, n_x, nb1):
    att_ref, cv_ref = refs[:2]
    x_refs = refs[2:2 + n_x]
    w_ref, g_ref, b_ref, wr_ref, br_ref = refs[2 + n_x:7 + n_x]
    mix = _dot(att_ref[...], w_ref[0:Q_WIDTH, :]) + _dot(cv_ref[...], w_ref[Q_WIDTH:, :])
    _norm_and_route(mix, _read_rows(x_refs, nb1), g_ref, b_ref, wr_ref, br_ref, refs[7 + n_x:], alpha)


def _route_outs(tt, d, tm, lag=0):
    nb = tt // tm
    row = lambda i: (jnp.clip(i - lag, 0, nb - 1), 0)
    specs = [pl.BlockSpec((tm, d), row), pl.BlockSpec((tm * SUBLANES, LANES), row),
             pl.BlockSpec((tm, LANES), row), pl.BlockSpec((tm, LANES), row)]
    shapes = [jax.ShapeDtypeStruct((tt, d), F32), jax.ShapeDtypeStruct((tt * SUBLANES, LANES), jnp.uint32),
              jax.ShapeDtypeStruct((tt, LANES), jnp.int32), jax.ShapeDtypeStruct((tt, LANES), F32)]
    return specs, shapes


def _outproj_a(att, cv, x_parts, w, g, b, wr, br, alpha):
    tt = att.shape[0]
    d = w.shape[1]
    tm = min(512, min(a.shape[0] for a in x_parts))
    row = lambda i: (i, 0)
    out_specs, out_shape = _route_outs(tt, d, tm)
    return pl.pallas_call(
        functools.partial(_outproj_a_kernel, alpha=alpha, n_x=len(x_parts), nb1=_nb1(x_parts, tm)),
        grid=(tt // tm,),
        in_specs=[pl.BlockSpec((tm, Q_WIDTH), row), pl.BlockSpec((tm, CONV_WIDTH), row)]
        + _row_specs(x_parts, tm, d)
        + [_resident(w.shape), _resident((1, d)), _resident((1, d)), _resident(wr.shape), _resident((1, LANES))],
        out_specs=out_specs, out_shape=out_shape,
        compiler_params=_params(),
        name="outproj_a",
    )(att, cv, *x_parts, w, g.reshape(1, d), b.reshape(1, d), wr, br)


def _inproj_c_kernel(x_ref, w_ref, bg_ref, cx_ref):
    xb = x_ref[...].astype(BF16)
    d = x_ref.shape[1]
    cw = 512
    for c in range(d // cw):
        cs = slice(cw * c, cw * (c + 1))
        bg_ref[:, cs] = _dot(xb, w_ref[:, cw * c:cw * (c + 1)])
        cg = _dot(xb, w_ref[:, d + cw * c:d + cw * (c + 1)])
        xv = _dot(xb, w_ref[:, 2 * d + cw * c:2 * d + cw * (c + 1)])
        cx_ref[:, cs] = cg * xv


def _inproj_c(x, w):
    tt, d = x.shape
    tm = min(256, tt)
    row = lambda i: (i, 0)
    return pl.pallas_call(
        _inproj_c_kernel,
        grid=(tt // tm,),
        in_specs=[pl.BlockSpec((tm, d), row), _resident(w.shape)],
        out_specs=[pl.BlockSpec((tm, d), row), pl.BlockSpec((tm, d), row)],
        out_shape=[jax.ShapeDtypeStruct((tt, d), F32), jax.ShapeDtypeStruct((tt, d), F32)],
        compiler_params=_params(),
        name="inproj_c",
    )(x, w)


def _outproj_c_kernel(bg_ref, cx_ref, cp_ref, cn_ref, cw_ref, x_ref, w_ref, g_ref, b_ref, wr_ref, br_ref,
                      *outs, alpha, dims, tm):
    tp, sp, ss = dims
    off, slen = _seq_offset(pl.program_id(0) * tm, tp, sp, ss)
    keep_prev = (off > 0).astype(F32)
    keep_next = (off + tm < slen).astype(F32)
    cx = cx_ref[...]
    first = cp_ref[SHORT_HALO - 1:SHORT_HALO, :] * keep_prev
    last = cn_ref[0:1, :] * keep_next
    rid = lax.broadcasted_iota(jnp.int32, cx.shape, 0)
    before = jnp.where(rid == 0, first, pltpu.roll(cx, 1, 0))
    after = jnp.where(rid == tm - 1, last, pltpu.roll(cx, tm - 1, 0))
    conv = before * cw_ref[0:1, :] + cx * cw_ref[1:2, :] + after * cw_ref[2:3, :]
    y = (bg_ref[...] * conv).astype(BF16)
    _norm_and_route(_dot(y, w_ref[...]), x_ref[...], g_ref, b_ref, wr_ref, br_ref, outs, alpha)


def _outproj_c(bg, cx, cw, x, w, g, b, wr, br, alpha, dims):
    tt, d = x.shape
    tp, sp, ss = dims
    tm = min(512, sp, ss)
    hb = tm // SHORT_HALO
    nh = tt // SHORT_HALO
    row = lambda i: (i, 0)
    out_specs, out_shape = _route_outs(tt, d, tm)
    return pl.pallas_call(
        functools.partial(_outproj_c_kernel, alpha=alpha, dims=dims, tm=tm),
        grid=(tt // tm,),
        in_specs=[pl.BlockSpec((tm, d), row), pl.BlockSpec((tm, d), row),
                  pl.BlockSpec((SHORT_HALO, d), lambda i: (jnp.maximum(i * hb - 1, 0), 0)),
                  pl.BlockSpec((SHORT_HALO, d), lambda i: (jnp.minimum((i + 1) * hb, nh - 1), 0)),
                  _resident(cw.shape), pl.BlockSpec((tm, d), row), _resident(w.shape),
                  _resident((1, d)), _resident((1, d)), _resident(wr.shape), _resident((1, LANES))],
        out_specs=out_specs, out_shape=out_shape,
        compiler_params=_params(),
        name="outproj_c",
    )(bg, cx, cx, cx, cw, x, w, g.reshape(1, d), b.reshape(1, d), wr, br)


def _plan_kernel(eid_ref, pos_ref, tmeta_ref, pbase_ref, cbase_ref, *, tile, n_tiles_pad, n_codes):
    phase = pl.program_id(0)
    t = pl.program_id(1)
    tp = eid_ref.shape[0]
    lane = lax.broadcasted_iota(jnp.int32, (tp, LANES), 1)
    oh0 = lane == eid_ref[:, 0:1]
    oh1 = lane == eid_ref[:, 1:2]
    chosen = jnp.where(oh0 | oh1, 1.0, 0.0)
    counts = jnp.sum(chosen, axis=0, keepdims=True)

    @pl.when((phase == 0) & (t == 0))
    def _():
        pbase_ref[...] = jnp.zeros_like(pbase_ref)

    @pl.when(phase == 0)
    def _():
        pbase_ref[...] += counts

    @pl.when((phase == 1) & (t == 0))
    def _():
        totals = pbase_ref[...]
        ntile = jnp.floor((totals + (tile - 1)) * (1.0 / tile))
        hi = jnp.floor(totals * (1.0 / 256.0))
        lo = totals - 256.0 * hi
        r = lax.broadcasted_iota(jnp.int32, (LANES, LANES), 0)
        c = lax.broadcasted_iota(jnp.int32, (LANES, LANES), 1)
        upper = jnp.where(r < c, 1.0, 0.0).astype(BF16)

        def prefix(v):
            return _dot(jnp.broadcast_to(v, (SUBLANES, LANES)).astype(BF16), upper)[0:1, :]

        tstart = prefix(ntile)
        cstart = 256.0 * prefix(hi) + prefix(lo)
        pbase_ref[...] = tstart * tile
        cbase_ref[...] = cstart
        j = lax.broadcasted_iota(jnp.int32, (n_tiles_pad, LANES), 0).astype(F32)
        ln = lax.broadcasted_iota(jnp.int32, (n_tiles_pad, LANES), 1)
        done = jnp.where((tstart + ntile <= j) & (ln < N_EXPERTS), 1.0, 0.0)
        texp = jnp.minimum(jnp.sum(done, axis=-1, keepdims=True), N_EXPERTS - 1.0)
        total = jnp.sum(jnp.where(ln[0:1, :] < N_EXPERTS, ntile, 0.0), axis=-1, keepdims=True)
        mine = ln.astype(F32) == texp
        pick = lambda v: jnp.sum(jnp.where(mine, v, 0.0), axis=-1, keepdims=True)
        first = pick(cstart) + (j[:, 0:1] - pick(tstart)) * tile
        first = jnp.clip(first, 0.0, n_codes - 1.0)
        seg_end = pick(tstart + ntile)
        seg_ord = pick(prefix(jnp.minimum(ntile, 1.0)))
        tmeta_ref[...] = jnp.where(ln == 1, total, jnp.where(ln == 2, first, jnp.where(
            ln == 3, seg_end, jnp.where(ln == 4, seg_ord, texp)))).astype(jnp.int32)

    @pl.when(phase == 1)
    def _():
        r = lax.broadcasted_iota(jnp.int32, (tp, tp), 0)
        c = lax.broadcasted_iota(jnp.int32, (tp, tp), 1)
        lower = jnp.where(c < r, 1.0, 0.0).astype(BF16)
        rank = _dot(lower, chosen.astype(BF16))
        pslot = pbase_ref[...] + rank
        cslot = cbase_ref[...] + rank
        pick = lambda oh, v: jnp.sum(jnp.where(oh, v, 0.0), axis=-1, keepdims=True)
        out = jnp.where(lane == 0, pick(oh0, pslot), jnp.where(lane == 1, pick(oh1, pslot),
              jnp.where(lane == 2, pick(oh0, cslot), jnp.where(lane == 3, pick(oh1, cslot), 0.0))))
        pos_ref[...] = out.T[0:SUBLANES, :].astype(jnp.int32)
        pbase_ref[...] += counts
        cbase_ref[...] += counts


def _plan(eid, n_tiles_pad):
    tt = eid.shape[0]
    tp = min(1024, tt)
    return pl.pallas_call(
        functools.partial(_plan_kernel, tile=MOE_TILE, n_tiles_pad=n_tiles_pad, n_codes=2 * tt),
        grid=(2, tt // tp),
        in_specs=[pl.BlockSpec((tp, LANES), lambda ph, t: (t, 0))],
        out_specs=[pl.BlockSpec((SUBLANES, tp), lambda ph, t: (0, t * ph)),
                   pl.BlockSpec((n_tiles_pad, LANES), lambda ph, t: (0, 0))],
        out_shape=[jax.ShapeDtypeStruct((SUBLANES, tt), jnp.int32),
                   jax.ShapeDtypeStruct((n_tiles_pad, LANES), jnp.int32)],
        scratch_shapes=[pltpu.VMEM((1, LANES), F32), pltpu.VMEM((1, LANES), F32)],
        compiler_params=_params(2),
        name="moe_plan",
    )(eid)


def _invert_kernel(cpos_ref, code_ref, *, tt, tok):
    step = pl.program_id(0)

    @pl.when(step == 0)
    def _():
        for r in range(MOE_TILE):
            code_ref[2 * tt + r] = 0

    def put(r, _):
        t = step * tok + r
        code_ref[cpos_ref[t]] = SUBLANES * t
        code_ref[cpos_ref[tt + t]] = SUBLANES * t
        return 0

    lax.fori_loop(0, tok, put, 0, unroll=DMA_UNROLL)


def _invert(cpos_flat, tt):
    tok = min(2048, tt)
    grid_spec = pltpu.PrefetchScalarGridSpec(
        num_scalar_prefetch=1, grid=(tt // tok,), in_specs=[],
        out_specs=pl.BlockSpec(memory_space=pltpu.SMEM))
    return pl.pallas_call(
        functools.partial(_invert_kernel, tt=tt, tok=tok),
        grid_spec=grid_spec,
        out_shape=jax.ShapeDtypeStruct((2 * tt + MOE_TILE,), jnp.int32),
        compiler_params=_params(),
        name="moe_invert",
    )(cpos_flat)


def _moe_kernel(texp_ref, nused_ref, first_ref, segend_ref, segord_ref, code_ref, hp_ref, wup_ref, wdn_ref,
                o_ref, xbuf, gsem, wup_f, wdn_f, wsem, wup_b, wdn_b, *, tm, layer):
    j = pl.program_id(0)
    nused = nused_ref[0]
    slot = j % 2

    def weight_copies(expert, par):
        e = layer * N_EXPERTS + expert
        return (pltpu.make_async_copy(wup_ref.at[e], wup_f.at[par], wsem.at[par]),
                pltpu.make_async_copy(wdn_ref.at[e], wdn_f.at[par], wsem.at[par]))

    def gather(tile, buf, rows):
        first = first_ref[tile]
        for r in rows:
            src = pl.multiple_of(code_ref[first + r], SUBLANES)
            pltpu.make_async_copy(hp_ref.at[pl.ds(src, SUBLANES)],
                                  xbuf.at[buf, pl.ds(SUBLANES * r, SUBLANES)],
                                  gsem.at[buf]).start(priority=r % 2)

    def wait_tile(buf):
        pltpu.make_async_copy(hp_ref.at[pl.ds(0, SUBLANES * tm)], xbuf.at[buf], gsem.at[buf]).wait()

    @pl.when(j == 0)
    def _():
        for c in weight_copies(texp_ref[0], 0):
            c.start(priority=1)
        gather(0, 0, range(tm))

    @pl.when(j < nused)
    def _():
        gather(j + 1, 1 - slot, range(tm))

    @pl.when((j < nused) & ((j == 0) | (texp_ref[j] != texp_ref[jnp.maximum(j - 1, 0)])))
    def _():
        par = segord_ref[j] % 2
        for c in weight_copies(texp_ref[j], par):
            c.wait()
        wup_b[...] = wup_f[par].astype(BF16)
        wdn_b[...] = wdn_f[par].astype(BF16)
        nxt = segend_ref[j]

        @pl.when(nxt < nused)
        def _():
            for c in weight_copies(texp_ref[nxt], 1 - par):
                c.start(priority=1)

    @pl.when(j < nused)
    def _():
        wait_tile(slot)
        xb = _unpack_rows(_load_token_tiles(xbuf.at[slot], tm)).astype(BF16)
        up = _dot(xb, wup_b[...])
        a = up[:, :D_EXPERT]
        hid = (a * jax.nn.sigmoid(a) * up[:, D_EXPERT:]).astype(BF16)
        _store_token_tiles(o_ref, _pack_rows(_dot(hid, wdn_b[...])))

    @pl.when(j == nused)
    def _():
        wait_tile(slot)

    @pl.when(j >= nused)
    def _():
        o_ref[...] = jnp.zeros_like(o_ref)


def _moe(tmeta, code, hp, wup, wdn, n_tiles, layer):
    tm = MOE_TILE
    any_spec = pl.BlockSpec(memory_space=pl.ANY)
    grid_spec = pltpu.PrefetchScalarGridSpec(
        num_scalar_prefetch=6,
        grid=(n_tiles + 1,),
        in_specs=[any_spec, any_spec, any_spec],
        out_specs=pl.BlockSpec((tm * SUBLANES, LANES), lambda j, *_: (j, 0)),
        scratch_shapes=[pltpu.VMEM((2, tm * SUBLANES, LANES), jnp.uint32), pltpu.SemaphoreType.DMA((2,)),
                        pltpu.VMEM((2,) + wup.shape[1:], F32), pltpu.VMEM((2,) + wdn.shape[1:], F32),
                        pltpu.SemaphoreType.DMA((2,)),
                        pltpu.VMEM(wup.shape[1:], BF16), pltpu.VMEM(wdn.shape[1:], BF16)],
    )
    return pl.pallas_call(
        functools.partial(_moe_kernel, tm=tm, layer=layer),
        grid_spec=grid_spec,
        out_shape=jax.ShapeDtypeStruct(((n_tiles + 1) * tm * SUBLANES, LANES), jnp.uint32),
        compiler_params=_params(),
        name="moe_experts",
    )(tmeta[:, 0], tmeta[0:1, 1], tmeta[:, 2], tmeta[:, 3], tmeta[:, 4], code, hp, wup, wdn)


def _combine_kernel(pos_ref, *refs, alpha, tm, tt, n_p, nb1, n_out):
    h_ref, ys_ref, wt_ref = refs[:3]
    p_refs = refs[3:3 + n_p]
    g_ref, b_ref, wg_ref, bg_ref, wp_ref = refs[3 + n_p:8 + n_p]
    o_refs = refs[8 + n_p:8 + n_p + n_out]
    ybuf, sem = refs[8 + n_p + n_out:]
    i = pl.program_id(0)
    slot = i % 2

    last = pl.num_programs(0) - 1

    def gather(step, buf, k, rows):
        base = k * tt + step * tm
        for r in rows:
            src = pl.multiple_of(pos_ref[base + r], SUBLANES)
            pltpu.make_async_copy(ys_ref.at[pl.ds(src, SUBLANES)],
                                  ybuf.at[buf, k, pl.ds(SUBLANES * r, SUBLANES)],
                                  sem.at[buf]).start(priority=r % 2)

    def wait_step(buf):
        for k in range(2):
            pltpu.make_async_copy(ys_ref.at[pl.ds(0, SUBLANES * tm)], ybuf.at[buf, k], sem.at[buf]).wait()

    @pl.when(i == 0)
    def _():
        for k in range(2):
            gather(0, 0, k, range(tm))

    @pl.when(i < last)
    def _():
        for k in range(2):
            gather(i + 1, 1 - slot, k, range(tm))

    wait_step(slot)
    wt = wt_ref[...]
    y0 = _unpack_rows(_load_token_tiles(ybuf.at[slot, 0], tm))
    y1 = _unpack_rows(_load_token_tiles(ybuf.at[slot, 1], tm))
    moe = y0 * wt[:, 0:1] + y1 * wt[:, 1:2]
    h2 = _layer_norm(alpha * h_ref[...] + moe, g_ref[...], b_ref[...])
    gate = jax.nn.sigmoid(_dot(h2.astype(BF16), wg_ref[...]) + bg_ref[...])
    x_new = h2 + gate * _dot(_read_rows(p_refs, nb1).astype(BF16), wp_ref[...])

    if n_out == 1:
        o_refs[0][...] = x_new
    else:
        @pl.when(i < nb1)
        def _():
            o_refs[0][...] = x_new

        @pl.when(i >= nb1)
        def _():
            o_refs[1][...] = x_new


def _combine(pos_flat, h, ys, wt, p_parts, g, b, wg, bg, wp, alpha, split_rows):
    tt, d = h.shape
    tm = min(256, min(a.shape[0] for a in p_parts))
    row = lambda i, *_: (i, 0)
    nb1 = _nb1(p_parts, tm)
    if split_rows is None:
        out_specs = pl.BlockSpec((tm, d), row)
        out_shape = jax.ShapeDtypeStruct((tt, d), F32)
        n_out = 1
    else:
        assert split_rows == p_parts[0].shape[0]
        out_specs = [pl.BlockSpec((tm, d), lambda i, *_: (jnp.minimum(i, nb1 - 1), 0)),
                     pl.BlockSpec((tm, d), lambda i, *_: (jnp.maximum(i - nb1, 0), 0))]
        out_shape = [jax.ShapeDtypeStruct((split_rows, d), F32), jax.ShapeDtypeStruct((tt - split_rows, d), F32)]
        n_out = 2
    grid_spec = pltpu.PrefetchScalarGridSpec(
        num_scalar_prefetch=1,
        grid=(tt // tm,),
        in_specs=[pl.BlockSpec((tm, d), row), pl.BlockSpec(memory_space=pl.ANY), pl.BlockSpec((tm, LANES), row)]
        + _row_specs(p_parts, tm, p_parts[0].shape[1])
        + [_resident((1, d)), _resident((1, d)), _resident(wg.shape), _resident((1, d)), _resident(wp.shape)],
        out_specs=out_specs,
        scratch_shapes=[pltpu.VMEM((2, 2, tm * SUBLANES, LANES), jnp.uint32), pltpu.SemaphoreType.DMA((2,))],
    )
    return pl.pallas_call(
        functools.partial(_combine_kernel, alpha=alpha, tm=tm, tt=tt, n_p=len(p_parts), nb1=nb1, n_out=n_out),
        grid_spec=grid_spec,
        out_shape=out_shape,
        compiler_params=_params(),
        name="moe_combine_ple",
    )(pos_flat, h, ys, wt, *p_parts, g.reshape(1, d), b.reshape(1, d), wg, bg.reshape(1, d), wp)


def _moe_block(hp, eid, wup, wdn, layer):
    tt = eid.shape[0]
    n_tiles = (2 * tt + N_EXPERTS * (MOE_TILE - 1)) // MOE_TILE
    n_tiles_pad = -(-(n_tiles + 1) // SUBLANES) * SUBLANES
    pos, tmeta = _plan(eid, n_tiles_pad)
    code = _invert(pos[2:4].reshape(-1), tt)
    ys = _moe(tmeta, code, hp, wup, wdn, n_tiles, layer)
    return ys, SUBLANES * pos[0:2].reshape(-1)


def kernel(x_prompt, x_sample, p_prompt, p_sample, w_in_a, sink, conv_dw_a, conv_b_a, conv_ln_g, conv_ln_b,
           w_out_a, w_in_c, conv_w_c, w_out_c, ln1_g, ln1_b, ln2_g, ln2_b, w_router_group, b_router_group,
           w_router_expert, b_router_expert, w_up, w_down, w_ple, w_ple_gate, b_ple_gate):
    bp, sp, d = x_prompt.shape
    bs, ss, _ = x_sample.shape
    depth = p_prompt.shape[0]
    tp = bp * sp
    ts = bs * ss
    dims = (tp, sp, ss)
    alpha = (2.0 * depth) ** 0.25

    x_parts = (x_prompt.reshape(tp, d), x_sample.reshape(ts, d))
    tabs = _rope_tables(max(sp, ss))
    pad = LANES - N_GROUPS - N_EXPERTS
    w_route = jnp.concatenate(
        [w_router_group, w_router_expert, jnp.zeros((depth, d, pad), F32)], axis=-1).astype(BF16)
    b_route = jnp.concatenate(
        [b_router_group, b_router_expert, jnp.zeros((depth, pad), F32)], axis=-1).reshape(depth, 1, LANES)

    w_up_all = w_up.reshape((depth * N_EXPERTS,) + w_up.shape[2:])
    w_down_all = w_down.reshape((depth * N_EXPERTS,) + w_down.shape[2:])

    for i in range(depth):
        j = i // 2
        wr, br = w_route[i], b_route[i]
        if i % 2 == 0:
            q, k, v, z = _inproj_a(x_parts, w_in_a[j].astype(BF16), tabs, dims)
            att = _attention(q, k, v, sink[j], dims)
            cv = _conv_a(z, conv_dw_a[j], conv_b_a[j], conv_ln_g[j], conv_ln_b[j], dims)
            h, hp, eid, wt = _outproj_a(att, cv, x_parts, w_out_a[j].astype(BF16), ln1_g[i], ln1_b[i], wr, br, alpha)
        else:
            bg, cx = _inproj_c(x_parts[0], w_in_c[j].astype(BF16))
            h, hp, eid, wt = _outproj_c(bg, cx, conv_w_c[j], x_parts[0], w_out_c[j].astype(BF16), ln1_g[i], ln1_b[i],
                                    wr, br, alpha, dims)
        ys, pos_flat = _moe_block(hp, eid, w_up_all, w_down_all, i)
        p_parts = (p_prompt[i].reshape(tp, -1), p_sample[i].reshape(ts, -1))
        out = _combine(pos_flat, h, ys, wt, p_parts, ln2_g[i], ln2_b[i], w_ple_gate[i].astype(BF16),
                       b_ple_gate[i], w_ple[i].astype(BF16), alpha, tp if i == depth - 1 else None)
        x_parts = (out,)

    y_prompt, y_sample = out
    return y_prompt.reshape(bp, sp, d), y_sample.reshape(bs, ss, d)
```

```python
import functools

import jax
import jax.numpy as jnp
from jax import lax
from jax.experimental import pallas as pl
from jax.experimental.pallas import tpu as pltpu

F32 = jnp.float32
BF16 = jnp.bfloat16

N_HEADS = 16
N_KV_HEADS = 4
HEAD_DIM = 64
Q_WIDTH = N_HEADS * HEAD_DIM
KV_WIDTH = N_KV_HEADS * HEAD_DIM
BLOCK = 128
ROT_DIM = HEAD_DIM // 4
ROPE_THETA = 500000.0
CONV_WIDTH = 1024
CONV_KSIZE = 31
CONV_HALO = 16
SHORT_HALO = 8
N_GROUPS = 4
EXPERTS_PER_GROUP = 8
N_EXPERTS = N_GROUPS * EXPERTS_PER_GROUP
D_EXPERT = 512
LN_EPS = 1e-5
NEG_INF = -1e30
LANES = 128
SUBLANES = 8

MOE_TILE = 256
DMA_UNROLL = 8
VMEM_LIMIT = 56 * 1024 * 1024


def _params(n_axes=1):
    return pltpu.CompilerParams(dimension_semantics=("arbitrary",) * n_axes,
                                vmem_limit_bytes=VMEM_LIMIT)


def _resident(shape):
    nd = len(shape)
    return pl.BlockSpec(shape, lambda *_: (0,) * nd, pipeline_mode=pl.Buffered(1))


def _dot(a, b):
    return jnp.dot(a, b, preferred_element_type=F32)


def _layer_norm(y, g, b):
    mu = jnp.mean(y, axis=-1, keepdims=True)
    d = y - mu
    var = jnp.mean(d * d, axis=-1, keepdims=True)
    return d * lax.rsqrt(var + LN_EPS) * g + b


def _seq_offset(r0, tp, sp, ss):
    in_p = r0 < tp
    off = jnp.where(in_p, r0 % sp, (r0 - tp) % ss)
    return off, jnp.where(in_p, sp, ss)


def _row_specs(parts, tm, width, lag=0):
    nb = sum(a.shape[0] for a in parts) // tm
    tile = lambda i: jnp.clip(i - lag, 0, nb - 1)
    if len(parts) == 1:
        return [pl.BlockSpec((tm, width), lambda i, *_: (tile(i), 0))]
    nb1 = parts[0].shape[0] // tm
    return [pl.BlockSpec((tm, width), lambda i, *_: (jnp.minimum(tile(i), nb1 - 1), 0)),
            pl.BlockSpec((tm, width), lambda i, *_: (jnp.maximum(tile(i) - nb1, 0), 0))]


def _read_rows(refs, nb1, lag=0):
    if len(refs) == 1:
        return refs[0][...]
    return jnp.where(pl.program_id(0) - lag < nb1, refs[0][...], refs[1][...])


def _nb1(parts, tm):
    return parts[0].shape[0] // tm


def _pack_rows(x):
    n = x.shape[1] // 2
    lo = lax.bitcast_convert_type(x[:, :n].astype(BF16).astype(F32), jnp.uint32) >> 16
    hi = lax.bitcast_convert_type(x[:, n:].astype(BF16).astype(F32), jnp.uint32) & jnp.uint32(0xFFFF0000)
    return lo | hi


def _unpack_rows(w):
    lo = lax.bitcast_convert_type(w << 16, F32)
    hi = lax.bitcast_convert_type(w & jnp.uint32(0xFFFF0000), F32)
    return jnp.concatenate([lo, hi], axis=1)


def _store_token_tiles(ref, w):
    m = w.shape[0]
    for s in range(SUBLANES):
        ref[pl.ds(s, m, stride=SUBLANES), :] = w[:, LANES * s:LANES * (s + 1)]


def _load_token_tiles(ref, m):
    return jnp.concatenate([ref[pl.ds(s, m, stride=SUBLANES), :] for s in range(SUBLANES)], axis=1)


def _inproj_a_kernel(*refs, n_x, nb1):
    x_refs, (w_ref, rc_ref, ra_ref, rb_ref, q_ref, k_ref, v_ref, z_ref) = refs[:n_x], refs[n_x:]
    xb = _read_rows(x_refs, nb1).astype(BF16)
    rc, ra, rb = rc_ref[...], ra_ref[...], rb_ref[...]

    def rope(u):
        outs = []
        half = ROT_DIM // 2
        for j in range(u.shape[1] // LANES):
            t = u[:, LANES * j:LANES * (j + 1)]
            outs.append(t * rc + pltpu.roll(t, LANES - half, 1) * ra + pltpu.roll(t, half, 1) * rb)
        return jnp.concatenate(outs, axis=1)

    cw = 512
    for c in range(Q_WIDTH // cw):
        u = _dot(xb, w_ref[:, cw * c:cw * (c + 1)])
        q_ref[cw * c:cw * (c + 1), :] = (rope(u) * (HEAD_DIM ** -0.5)).T.astype(BF16)
    u = _dot(xb, w_ref[:, Q_WIDTH:Q_WIDTH + 2 * KV_WIDTH])
    k_ref[...] = rope(u[:, :KV_WIDTH]).astype(BF16)
    v_ref[...] = u[:, KV_WIDTH:].T.astype(BF16)
    base = Q_WIDTH + 2 * KV_WIDTH
    for c in range(CONV_WIDTH // cw):
        a = _dot(xb, w_ref[:, base + cw * c:base + cw * (c + 1)])
        g = _dot(xb, w_ref[:, base + CONV_WIDTH + cw * c:base + CONV_WIDTH + cw * (c + 1)])
        z_ref[:, cw * c:cw * (c + 1)] = a * jax.nn.sigmoid(g)


def _rope_tables(smax):
    half = ROT_DIM // 2
    inv_freq = ROPE_THETA ** (-(jnp.arange(half, dtype=F32) * 2.0 / ROT_DIM))
    ang = jnp.arange(smax, dtype=F32)[:, None] * inv_freq[None, :]
    cos, sin = jnp.cos(ang), jnp.sin(ang)
    d = jnp.arange(LANES) % HEAD_DIM
    f = d % half
    rc = jnp.where(d[None, :] < ROT_DIM, cos[:, f], 1.0)
    ra = jnp.where(d[None, :] < half, -sin[:, f], 0.0)
    rb = jnp.where((d[None, :] >= half) & (d[None, :] < ROT_DIM), sin[:, f], 0.0)
    return rc.astype(F32), ra.astype(F32), rb.astype(F32)


def _inproj_a(x_parts, w, tabs, dims):
    tp, sp, ss = dims
    tt = sum(a.shape[0] for a in x_parts)
    d = x_parts[0].shape[1]
    tm = min(512, sp, ss)
    n_in = w.shape[1]

    def tab_map(i):
        off, _ = _seq_offset(i * tm, tp, sp, ss)
        return (off // tm, 0)

    row = lambda i: (i, 0)
    col = lambda i: (0, i)
    tab_spec = pl.BlockSpec((tm, LANES), tab_map)
    return pl.pallas_call(
        functools.partial(_inproj_a_kernel, n_x=len(x_parts), nb1=_nb1(x_parts, tm)),
        grid=(tt // tm,),
        in_specs=_row_specs(x_parts, tm, d) + [_resident((d, n_in)), tab_spec, tab_spec, tab_spec],
        out_specs=[pl.BlockSpec((Q_WIDTH, tm), col), pl.BlockSpec((tm, KV_WIDTH), row),
                   pl.BlockSpec((KV_WIDTH, tm), col), pl.BlockSpec((tm, CONV_WIDTH), row)],
        out_shape=[jax.ShapeDtypeStruct((Q_WIDTH, tt), BF16), jax.ShapeDtypeStruct((tt, KV_WIDTH), BF16),
                   jax.ShapeDtypeStruct((KV_WIDTH, tt), BF16), jax.ShapeDtypeStruct((tt, CONV_WIDTH), F32)],
        compiler_params=_params(),
        name="inproj_a",
    )(*x_parts, w, *tabs)


def _attn_kernel(sink_ref, q_ref, kp_ref, kc_ref, kn_ref, vp_ref, vc_ref, vn_ref, o_ref, ot_scr, *, dims):
    tp, sp, ss = dims
    off, slen = _seq_offset(pl.program_id(0) * BLOCK, tp, sp, ss)
    grp = N_HEADS // N_KV_HEADS
    no_prev = jnp.where(off > 0, 0, BLOCK)
    no_next = jnp.where(off + BLOCK < slen, 0, BLOCK)
    key = lax.broadcasted_iota(jnp.int32, (3 * BLOCK, grp * BLOCK), 0)
    qry = lax.broadcasted_iota(jnp.int32, (3 * BLOCK, grp * BLOCK), 1) % BLOCK
    jj = key % BLOCK
    blk = key // BLOCK
    valid = ((blk == 1) | ((blk == 0) & (jj >= qry + no_prev)) | ((blk == 2) & (jj + no_next <= qry)))
    kcat = jnp.concatenate([kp_ref[...], kc_ref[...], kn_ref[...]], axis=0)
    vtcat = jnp.concatenate([vp_ref[...], vc_ref[...], vn_ref[...]], axis=1)
    for h in range(N_KV_HEADS):
        heads = range(grp * h, grp * (h + 1))
        qw = jnp.concatenate([q_ref[HEAD_DIM * hh:HEAD_DIM * (hh + 1), :] for hh in heads], axis=1)
        sk = jnp.concatenate([jnp.full((1, BLOCK), sink_ref[hh], F32) for hh in heads], axis=1)
        s = _dot(kcat[:, HEAD_DIM * h:HEAD_DIM * (h + 1)], qw)
        s = jnp.where(valid, s, NEG_INF)
        m = jnp.maximum(jnp.max(s, axis=0, keepdims=True), sk)
        p = jnp.exp(s - m)
        denom = jnp.sum(p, axis=0, keepdims=True) + jnp.exp(sk - m)
        ot = _dot(vtcat[HEAD_DIM * h:HEAD_DIM * (h + 1), :], p.astype(BF16)) / denom
        for g, hh in enumerate(heads):
            ot_scr[HEAD_DIM * hh:HEAD_DIM * (hh + 1), :] = ot[:, BLOCK * g:BLOCK * (g + 1)]
    o_ref[...] = ot_scr[...].T.astype(BF16)


def _attention(qt, k, vt, sink, dims):
    tt = k.shape[0]
    nb = tt // BLOCK
    cur = lambda i: (i, 0)
    prv = lambda i: (jnp.maximum(i - 1, 0), 0)
    nxt = lambda i: (jnp.minimum(i + 1, nb - 1), 0)
    tr = lambda m: (lambda i: m(i)[::-1])
    kb = lambda m: pl.BlockSpec((BLOCK, KV_WIDTH), m)
    vb = lambda m: pl.BlockSpec((KV_WIDTH, BLOCK), tr(m))
    return pl.pallas_call(
        functools.partial(_attn_kernel, dims=dims),
        grid=(nb,),
        in_specs=[pl.BlockSpec(memory_space=pltpu.SMEM), pl.BlockSpec((Q_WIDTH, BLOCK), tr(cur)),
                  kb(prv), kb(cur), kb(nxt), vb(prv), vb(cur), vb(nxt)],
        out_specs=pl.BlockSpec((BLOCK, Q_WIDTH), cur),
        out_shape=jax.ShapeDtypeStruct((tt, Q_WIDTH), BF16),
        scratch_shapes=[pltpu.VMEM((Q_WIDTH, BLOCK), F32)],
        compiler_params=_params(),
        name="band_attention",
    )(sink.astype(F32), qt, k, k, k, vt, vt, vt)


def _conv_a_kernel(z_ref, zp_ref, zn_ref, w_ref, b_ref, g_ref, beta_ref, o_ref, ext_ref, *, dims, ts):
    tp, sp, ss = dims
    off, slen = _seq_offset(pl.program_id(0) * ts, tp, sp, ss)
    keep_prev = (off > 0).astype(F32)
    keep_next = (off + ts < slen).astype(F32)
    ext_ref[0:CONV_HALO, :] = zp_ref[...] * keep_prev
    ext_ref[CONV_HALO:CONV_HALO + ts, :] = z_ref[...]
    ext_ref[CONV_HALO + ts:, :] = zn_ref[...] * keep_next
    rows, lw = 128, 128
    n_a = -(-CONV_KSIZE // SUBLANES)
    for rc in range(ts // rows):
        r0 = rows * rc
        accs = []
        for lc in range(CONV_WIDTH // lw):
            ls = slice(lw * lc, lw * (lc + 1))
            acc = jnp.zeros((rows, lw), F32)
            for b in range(SUBLANES):
                part = jnp.zeros((rows + SUBLANES, lw), F32)
                for a in range(n_a):
                    k = SUBLANES * a + b
                    if k < CONV_KSIZE:
                        lo = r0 + SUBLANES * a
                        part = part + ext_ref[lo:lo + rows + SUBLANES, ls] * w_ref[k:k + 1, ls]
                acc = acc + part[b + 1:b + 1 + rows, :]
            accs.append(acc + b_ref[:, ls])
        y = _layer_norm(jnp.concatenate(accs, axis=1), g_ref[...], beta_ref[...])
        o_ref[r0:r0 + rows, :] = (y * jax.nn.sigmoid(y)).astype(BF16)


def _conv_a(z, w, b, g, beta, dims):
    tt, cw = z.shape
    tp, sp, ss = dims
    ts = min(256, sp, ss)
    hb = ts // CONV_HALO
    nh = tt // CONV_HALO
    vec = lambda a: a.reshape(1, cw)
    return pl.pallas_call(
        functools.partial(_conv_a_kernel, dims=dims, ts=ts),
        grid=(tt // ts,),
        in_specs=[pl.BlockSpec((ts, cw), lambda i: (i, 0)),
                  pl.BlockSpec((CONV_HALO, cw), lambda i: (jnp.maximum(i * hb - 1, 0), 0)),
                  pl.BlockSpec((CONV_HALO, cw), lambda i: (jnp.minimum((i + 1) * hb, nh - 1), 0)),
                  _resident((CONV_KSIZE, cw)), _resident((1, cw)), _resident((1, cw)), _resident((1, cw))],
        out_specs=pl.BlockSpec((ts, cw), lambda i: (i, 0)),
        out_shape=jax.ShapeDtypeStruct((tt, cw), BF16),
        scratch_shapes=[pltpu.VMEM((ts + 2 * CONV_HALO, cw), F32)],
        compiler_params=_params(),
        name="conformer_conv",
    )(z, z, z, w, vec(b), vec(g), vec(beta))


def _norm_and_route(mix, x, g_ref, b_ref, wr_ref, br_ref, outs, alpha):
    h_ref, hp_ref, eid_ref, wt_ref = outs
    h = _layer_norm(alpha * x + mix, g_ref[...], b_ref[...])
    h_ref[...] = h
    _store_token_tiles(hp_ref, _pack_rows(h))
    logits = _dot(h.astype(BF16), wr_ref[...]) + br_ref[...]
    lane = lax.broadcasted_iota(jnp.int32, logits.shape, 1).astype(F32)
    big = jnp.float32(LANES)
    ninf = jnp.float32(-jnp.inf)
    gmask = lane < N_GROUPS
    lg = jnp.where(gmask, logits, ninf)
    gmax = jnp.max(lg, axis=-1, keepdims=True)
    gidx = jnp.min(jnp.where(lg == gmax, lane, big), axis=-1, keepdims=True)
    gsum = jnp.sum(jnp.where(gmask, jnp.exp(logits - gmax), 0.0), axis=-1, keepdims=True)
    lo = N_GROUPS + EXPERTS_PER_GROUP * gidx
    le = jnp.where((lane >= lo) & (lane < lo + EXPERTS_PER_GROUP), logits, ninf)
    t0 = jnp.max(le, axis=-1, keepdims=True)
    i0 = jnp.min(jnp.where(le == t0, lane, big), axis=-1, keepdims=True)
    le1 = jnp.where(lane == i0, ninf, le)
    t1 = jnp.max(le1, axis=-1, keepdims=True)
    i1 = jnp.min(jnp.where(le1 == t1, lane, big), axis=-1, keepdims=True)
    r = jnp.exp(t1 - t0)
    w0 = 1.0 / ((1.0 + r) * gsum)
    w1 = r * w0
    eid_ref[...] = jnp.where(lane == 0, i0 - N_GROUPS, jnp.where(lane == 1, i1 - N_GROUPS, 0.0)).astype(jnp.int32)
    wt_ref[...] = jnp.where(lane == 0, w0, jnp.where(lane == 1, w1, 0.0))


def _outproj_a_kernel(*refs, alpha, n_x, nb1):
    att_ref, cv_ref = refs[:2]
    x_refs = refs[2:2 + n_x]
    w_ref, g_ref, b_ref, wr_ref, br_ref = refs[2 + n_x:7 + n_x]
    mix = _dot(att_ref[...], w_ref[0:Q_WIDTH, :]) + _dot(cv_ref[...], w_ref[Q_WIDTH:, :])
    _norm_and_route(mix, _read_rows(x_refs, nb1), g_ref, b_ref, wr_ref, br_ref, refs[7 + n_x:], alpha)


def _route_outs(tt, d, tm, lag=0):
    nb = tt // tm
    row = lambda i: (jnp.clip(i - lag, 0, nb - 1), 0)
    specs = [pl.BlockSpec((tm, d), row), pl.BlockSpec((tm * SUBLANES, LANES), row),
             pl.BlockSpec((tm, LANES), row), pl.BlockSpec((tm, LANES), row)]
    shapes = [jax.ShapeDtypeStruct((tt, d), F32), jax.ShapeDtypeStruct((tt * SUBLANES, LANES), jnp.uint32),
              jax.ShapeDtypeStruct((tt, LANES), jnp.int32), jax.ShapeDtypeStruct((tt, LANES), F32)]
    return specs, shapes


def _outproj_a(att, cv, x_parts, w, g, b, wr, br, alpha):
    tt = att.shape[0]
    d = w.shape[1]
    tm = min(512, min(a.shape[0] for a in x_parts))
    row = lambda i: (i, 0)
    out_specs, out_shape = _route_outs(tt, d, tm)
    return pl.pallas_call(
        functools.partial(_outproj_a_kernel, alpha=alpha, n_x=len(x_parts), nb1=_nb1(x_parts, tm)),
        grid=(tt // tm,),
        in_specs=[pl.BlockSpec((tm, Q_WIDTH), row), pl.BlockSpec((tm, CONV_WIDTH), row)]
        + _row_specs(x_parts, tm, d)
        + [_resident(w.shape), _resident((1, d)), _resident((1, d)), _resident(wr.shape), _resident((1, LANES))],
        out_specs=out_specs, out_shape=out_shape,
        compiler_params=_params(),
        name="outproj_a",
    )(att, cv, *x_parts, w, g.reshape(1, d), b.reshape(1, d), wr, br)


def _inproj_c_kernel(x_ref, w_ref, bg_ref, cx_ref):
    xb = x_ref[...].astype(BF16)
    d = x_ref.shape[1]
    cw = 512
    for c in range(d // cw):
        cs = slice(cw * c, cw * (c + 1))
        bg_ref[:, cs] = _dot(xb, w_ref[:, cw * c:cw * (c + 1)])
        cg = _dot(xb, w_ref[:, d + cw * c:d + cw * (c + 1)])
        xv = _dot(xb, w_ref[:, 2 * d + cw * c:2 * d + cw * (c + 1)])
        cx_ref[:, cs] = cg * xv


def _inproj_c(x, w):
    tt, d = x.shape
    tm = min(256, tt)
    row = lambda i: (i, 0)
    return pl.pallas_call(
        _inproj_c_kernel,
        grid=(tt // tm,),
        in_specs=[pl.BlockSpec((tm, d), row), _resident(w.shape)],
        out_specs=[pl.BlockSpec((tm, d), row), pl.BlockSpec((tm, d), row)],
        out_shape=[jax.ShapeDtypeStruct((tt, d), F32), jax.ShapeDtypeStruct((tt, d), F32)],
        compiler_params=_params(),
        name="inproj_c",
    )(x, w)


def _outproj_c_kernel(bg_ref, cx_ref, cp_ref, cn_ref, cw_ref, x_ref, w_ref, g_ref, b_ref, wr_ref, br_ref,
                      *outs, alpha, dims, tm):
    tp, sp, ss = dims
    off, slen = _seq_offset(pl.program_id(0) * tm, tp, sp, ss)
    keep_prev = (off > 0).astype(F32)
    keep_next = (off + tm < slen).astype(F32)
    cx = cx_ref[...]
    first = cp_ref[SHORT_HALO - 1:SHORT_HALO, :] * keep_prev
    last = cn_ref[0:1, :] * keep_next
    rid = lax.broadcasted_iota(jnp.int32, cx.shape, 0)
    before = jnp.where(rid == 0, first, pltpu.roll(cx, 1, 0))
    after = jnp.where(rid == tm - 1, last, pltpu.roll(cx, tm - 1, 0))
    conv = before * cw_ref[0:1, :] + cx * cw_ref[1:2, :] + after * cw_ref[2:3, :]
    y = (bg_ref[...] * conv).astype(BF16)
    _norm_and_route(_dot(y, w_ref[...]), x_ref[...], g_ref, b_ref, wr_ref, br_ref, outs, alpha)


def _outproj_c(bg, cx, cw, x, w, g, b, wr, br, alpha, dims):
    tt, d = x.shape
    tp, sp, ss = dims
    tm = min(512, sp, ss)
    hb = tm // SHORT_HALO
    nh = tt // SHORT_HALO
    row = lambda i: (i, 0)
    out_specs, out_shape = _route_outs(tt, d, tm)
    return pl.pallas_call(
        functools.partial(_outproj_c_kernel, alpha=alpha, dims=dims, tm=tm),
        grid=(tt // tm,),
        in_specs=[pl.BlockSpec((tm, d), row), pl.BlockSpec((tm, d), row),
                  pl.BlockSpec((SHORT_HALO, d), lambda i: (jnp.maximum(i * hb - 1, 0), 0)),
                  pl.BlockSpec((SHORT_HALO, d), lambda i: (jnp.minimum((i + 1) * hb, nh - 1), 0)),
                  _resident(cw.shape), pl.BlockSpec((tm, d), row), _resident(w.shape),
                  _resident((1, d)), _resident((1, d)), _resident(wr.shape), _resident((1, LANES))],
        out_specs=out_specs, out_shape=out_shape,
        compiler_params=_params(),
        name="outproj_c",
    )(bg, cx, cx, cx, cw, x, w, g.reshape(1, d), b.reshape(1, d), wr, br)


def _plan_kernel(eid_ref, pos_ref, tmeta_ref, pbase_ref, cbase_ref, *, tile, n_tiles_pad, n_codes):
    phase = pl.program_id(0)
    t = pl.program_id(1)
    tp = eid_ref.shape[0]
    lane = lax.broadcasted_iota(jnp.int32, (tp, LANES), 1)
    oh0 = lane == eid_ref[:, 0:1]
    oh1 = lane == eid_ref[:, 1:2]
    chosen = jnp.where(oh0 | oh1, 1.0, 0.0)
    counts = jnp.sum(chosen, axis=0, keepdims=True)

    @pl.when((phase == 0) & (t == 0))
    def _():
        pbase_ref[...] = jnp.zeros_like(pbase_ref)

    @pl.when(phase == 0)
    def _():
        pbase_ref[...] += counts

    @pl.when((phase == 1) & (t == 0))
    def _():
        totals = pbase_ref[...]
        ntile = jnp.floor((totals + (tile - 1)) * (1.0 / tile))
        hi = jnp.floor(totals * (1.0 / 256.0))
        lo = totals - 256.0 * hi
        r = lax.broadcasted_iota(jnp.int32, (LANES, LANES), 0)
        c = lax.broadcasted_iota(jnp.int32, (LANES, LANES), 1)
        upper = jnp.where(r < c, 1.0, 0.0).astype(BF16)

        def prefix(v):
            return _dot(jnp.broadcast_to(v, (SUBLANES, LANES)).astype(BF16), upper)[0:1, :]

        tstart = prefix(ntile)
        cstart = 256.0 * prefix(hi) + prefix(lo)
        pbase_ref[...] = tstart * tile
        cbase_ref[...] = cstart
        j = lax.broadcasted_iota(jnp.int32, (n_tiles_pad, LANES), 0).astype(F32)
        ln = lax.broadcasted_iota(jnp.int32, (n_tiles_pad, LANES), 1)
        done = jnp.where((tstart + ntile <= j) & (ln < N_EXPERTS), 1.0, 0.0)
        texp = jnp.minimum(jnp.sum(done, axis=-1, keepdims=True), N_EXPERTS - 1.0)
        total = jnp.sum(jnp.where(ln[0:1, :] < N_EXPERTS, ntile, 0.0), axis=-1, keepdims=True)
        mine = ln.astype(F32) == texp
        pick = lambda v: jnp.sum(jnp.where(mine, v, 0.0), axis=-1, keepdims=True)
        first = pick(cstart) + (j[:, 0:1] - pick(tstart)) * tile
        first = jnp.clip(first, 0.0, n_codes - 1.0)
        seg_end = pick(tstart + ntile)
        seg_ord = pick(prefix(jnp.minimum(ntile, 1.0)))
        tmeta_ref[...] = jnp.where(ln == 1, total, jnp.where(ln == 2, first, jnp.where(
            ln == 3, seg_end, jnp.where(ln == 4, seg_ord, texp)))).astype(jnp.int32)

    @pl.when(phase == 1)
    def _():
        r = lax.broadcasted_iota(jnp.int32, (tp, tp), 0)
        c = lax.broadcasted_iota(jnp.int32, (tp, tp), 1)
        lower = jnp.where(c < r, 1.0, 0.0).astype(BF16)
        rank = _dot(lower, chosen.astype(BF16))
        pslot = pbase_ref[...] + rank
        cslot = cbase_ref[...] + rank
        pick = lambda oh, v: jnp.sum(jnp.where(oh, v, 0.0), axis=-1, keepdims=True)
        out = jnp.where(lane == 0, pick(oh0, pslot), jnp.where(lane == 1, pick(oh1, pslot),
              jnp.where(lane == 2, pick(oh0, cslot), jnp.where(lane == 3, pick(oh1, cslot), 0.0))))
        pos_ref[...] = out.T[0:SUBLANES, :].astype(jnp.int32)
        pbase_ref[...] += counts
        cbase_ref[...] += counts


def _plan(eid, n_tiles_pad):
    tt = eid.shape[0]
    tp = min(1024, tt)
    return pl.pallas_call(
        functools.partial(_plan_kernel, tile=MOE_TILE, n_tiles_pad=n_tiles_pad, n_codes=2 * tt),
        grid=(2, tt // tp),
        in_specs=[pl.BlockSpec((tp, LANES), lambda ph, t: (t, 0))],
        out_specs=[pl.BlockSpec((SUBLANES, tp), lambda ph, t: (0, t * ph)),
                   pl.BlockSpec((n_tiles_pad, LANES), lambda ph, t: (0, 0))],
        out_shape=[jax.ShapeDtypeStruct((SUBLANES, tt), jnp.int32),
                   jax.ShapeDtypeStruct((n_tiles_pad, LANES), jnp.int32)],
        scratch_shapes=[pltpu.VMEM((1, LANES), F32), pltpu.VMEM((1, LANES), F32)],
        compiler_params=_params(2),
        name="moe_plan",
    )(eid)


def _invert_kernel(cpos_ref, code_ref, *, tt, tok):
    step = pl.program_id(0)

    @pl.when(step == 0)
    def _():
        for r in range(MOE_TILE):
            code_ref[2 * tt + r] = 0

    def put(r, _):
        t = step * tok + r
        code_ref[cpos_ref[t]] = SUBLANES * t
        code_ref[cpos_ref[tt + t]] = SUBLANES * t
        return 0

    lax.fori_loop(0, tok, put, 0, unroll=DMA_UNROLL)


def _invert(cpos_flat, tt):
    tok = min(2048, tt)
    grid_spec = pltpu.PrefetchScalarGridSpec(
        num_scalar_prefetch=1, grid=(tt // tok,), in_specs=[],
        out_specs=pl.BlockSpec(memory_space=pltpu.SMEM))
    return pl.pallas_call(
        functools.partial(_invert_kernel, tt=tt, tok=tok),
        grid_spec=grid_spec,
        out_shape=jax.ShapeDtypeStruct((2 * tt + MOE_TILE,), jnp.int32),
        compiler_params=_params(),
        name="moe_invert",
    )(cpos_flat)


def _moe_kernel(texp_ref, nused_ref, first_ref, segend_ref, segord_ref, code_ref, hp_ref, wup_ref, wdn_ref,
                o_ref, xbuf, gsem, wup_f, wdn_f, wsem, wup_b, wdn_b, anchor, *, tm, layer):
    j = pl.program_id(0)
    nused = nused_ref[0]
    slot = j % 2

    def weight_copies(expert, par):
        e = layer * N_EXPERTS + expert
        return (pltpu.make_async_copy(wup_ref.at[e], wup_f.at[par], wsem.at[par]),
                pltpu.make_async_copy(wdn_ref.at[e], wdn_f.at[par], wsem.at[par]))

    def gather(tile, buf, rows):
        first = first_ref[tile]
        for r in rows:
            src = pl.multiple_of(code_ref[first + r], SUBLANES)
            pltpu.make_async_copy(hp_ref.at[pl.ds(src, SUBLANES)],
                                  xbuf.at[buf, pl.ds(SUBLANES * r, SUBLANES)],
                                  gsem.at[buf]).start(priority=r % 2)

    def wait_tile(buf):
        pltpu.make_async_copy(hp_ref.at[pl.ds(0, SUBLANES * tm)], xbuf.at[buf], gsem.at[buf]).wait()

    @pl.when(j == 0)
    def _():
        for c in weight_copies(texp_ref[0], 0):
            c.start(priority=1)
        gather(0, 0, range(tm))


    @pl.when((j < nused) & ((j == 0) | (texp_ref[j] != texp_ref[jnp.maximum(j - 1, 0)])))
    def _():
        par = segord_ref[j] % 2
        for c in weight_copies(texp_ref[j], par):
            c.wait()
        wup_b[...] = wup_f[par].astype(BF16)
        wdn_b[0] = wdn_f[par].astype(BF16)
        nxt = segend_ref[j]

        @pl.when(nxt < nused)
        def _():
            for c in weight_copies(texp_ref[nxt], 1 - par):
                c.start(priority=1)

    @pl.when(j < nused)
    def _():
        wait_tile(slot)
        xb = _unpack_rows(_load_token_tiles(xbuf.at[slot], tm)).astype(BF16)
        up = _dot(xb, wup_b[...])
        gather(j + 1, 1 - slot, range(tm))
        z = lax.shift_right_logical(pl.semaphore_read(anchor), 31)
        a = up[:, :D_EXPERT]
        hid = (a * jax.nn.sigmoid(a) * up[:, D_EXPERT:]).astype(BF16)
        _store_token_tiles(o_ref, _pack_rows(_dot(hid, wdn_b[z])))

    @pl.when(j == nused)
    def _():
        wait_tile(slot)

    @pl.when(j >= nused)
    def _():
        o_ref[...] = jnp.zeros_like(o_ref)


def _moe(tmeta, code, hp, wup, wdn, n_tiles, layer):
    tm = MOE_TILE
    any_spec = pl.BlockSpec(memory_space=pl.ANY)
    grid_spec = pltpu.PrefetchScalarGridSpec(
        num_scalar_prefetch=6,
        grid=(n_tiles + 1,),
        in_specs=[any_spec, any_spec, any_spec],
        out_specs=pl.BlockSpec((tm * SUBLANES, LANES), lambda j, *_: (j, 0)),
        scratch_shapes=[pltpu.VMEM((2, tm * SUBLANES, LANES), jnp.uint32), pltpu.SemaphoreType.DMA((2,)),
                        pltpu.VMEM((2,) + wup.shape[1:], F32), pltpu.VMEM((2,) + wdn.shape[1:], F32),
                        pltpu.SemaphoreType.DMA((2,)),
                        pltpu.VMEM(wup.shape[1:], BF16), pltpu.VMEM((1,) + wdn.shape[1:], BF16),
                        pltpu.SemaphoreType.REGULAR],
    )
    return pl.pallas_call(
        functools.partial(_moe_kernel, tm=tm, layer=layer),
        grid_spec=grid_spec,
        out_shape=jax.ShapeDtypeStruct(((n_tiles + 1) * tm * SUBLANES, LANES), jnp.uint32),
        compiler_params=_params(),
        name="moe_experts",
    )(tmeta[:, 0], tmeta[0:1, 1], tmeta[:, 2], tmeta[:, 3], tmeta[:, 4], code, hp, wup, wdn)


def _combine_kernel(pos_ref, *refs, alpha, tm, tt, n_p, nb1, n_out):
    h_ref, ys_ref, wt_ref = refs[:3]
    p_refs = refs[3:3 + n_p]
    g_ref, b_ref, wg_ref, bg_ref, wp_ref = refs[3 + n_p:8 + n_p]
    o_refs = refs[8 + n_p:8 + n_p + n_out]
    ybuf, sem, anchor = refs[8 + n_p + n_out:]
    i = pl.program_id(0)
    slot = i % 2

    last = pl.num_programs(0) - 1

    def gather(step, buf, k, rows):
        base = k * tt + step * tm
        for r in rows:
            src = pl.multiple_of(pos_ref[base + r], SUBLANES)
            pltpu.make_async_copy(ys_ref.at[pl.ds(src, SUBLANES)],
                                  ybuf.at[buf, k, pl.ds(SUBLANES * r, SUBLANES)],
                                  sem.at[buf]).start(priority=r % 2)

    def wait_step(buf):
        for k in range(2):
            pltpu.make_async_copy(ys_ref.at[pl.ds(0, SUBLANES * tm)], ybuf.at[buf, k], sem.at[buf]).wait()

    @pl.when(i == 0)
    def _():
        for k in range(2):
            gather(0, 0, k, range(tm))

    wait_step(slot)
    wt = wt_ref[...]
    y0 = _unpack_rows(_load_token_tiles(ybuf.at[slot, 0], tm))
    y1 = _unpack_rows(_load_token_tiles(ybuf.at[slot, 1], tm))
    moe = y0 * wt[:, 0:1] + y1 * wt[:, 1:2]
    h2 = _layer_norm(alpha * h_ref[...] + moe, g_ref[...], b_ref[...])
    gate = jax.nn.sigmoid(_dot(h2.astype(BF16), wg_ref[...]) + bg_ref[...])
    nxt = jnp.minimum(i + 1, last)
    for k in range(2):
        gather(nxt, 1 - slot, k, range(tm))
    z = lax.shift_right_logical(pl.semaphore_read(anchor), 31)
    x_new = h2 + gate * _dot(_read_rows(p_refs, nb1).astype(BF16), wp_ref[z])

    @pl.when(i == last)
    def _():
        wait_step(1 - slot)

    if n_out == 1:
        o_refs[0][...] = x_new
    else:
        @pl.when(i < nb1)
        def _():
            o_refs[0][...] = x_new

        @pl.when(i >= nb1)
        def _():
            o_refs[1][...] = x_new


def _combine(pos_flat, h, ys, wt, p_parts, g, b, wg, bg, wp, alpha, split_rows):
    tt, d = h.shape
    tm = min(256, min(a.shape[0] for a in p_parts))
    row = lambda i, *_: (i, 0)
    nb1 = _nb1(p_parts, tm)
    if split_rows is None:
        out_specs = pl.BlockSpec((tm, d), row)
        out_shape = jax.ShapeDtypeStruct((tt, d), F32)
        n_out = 1
    else:
        assert split_rows == p_parts[0].shape[0]
        out_specs = [pl.BlockSpec((tm, d), lambda i, *_: (jnp.minimum(i, nb1 - 1), 0)),
                     pl.BlockSpec((tm, d), lambda i, *_: (jnp.maximum(i - nb1, 0), 0))]
        out_shape = [jax.ShapeDtypeStruct((split_rows, d), F32), jax.ShapeDtypeStruct((tt - split_rows, d), F32)]
        n_out = 2
    grid_spec = pltpu.PrefetchScalarGridSpec(
        num_scalar_prefetch=1,
        grid=(tt // tm,),
        in_specs=[pl.BlockSpec((tm, d), row), pl.BlockSpec(memory_space=pl.ANY), pl.BlockSpec((tm, LANES), row)]
        + _row_specs(p_parts, tm, p_parts[0].shape[1])
        + [_resident((1, d)), _resident((1, d)), _resident(wg.shape), _resident((1, d)),
           _resident((1,) + wp.shape)],
        out_specs=out_specs,
        scratch_shapes=[pltpu.VMEM((2, 2, tm * SUBLANES, LANES), jnp.uint32), pltpu.SemaphoreType.DMA((2,)),
                        pltpu.SemaphoreType.REGULAR],
    )
    return pl.pallas_call(
        functools.partial(_combine_kernel, alpha=alpha, tm=tm, tt=tt, n_p=len(p_parts), nb1=nb1, n_out=n_out),
        grid_spec=grid_spec,
        out_shape=out_shape,
        compiler_params=_params(),
        name="moe_combine_ple",
    )(pos_flat, h, ys, wt, *p_parts, g.reshape(1, d), b.reshape(1, d), wg, bg.reshape(1, d), wp[None])


def _moe_block(hp, eid, wup, wdn, layer):
    tt = eid.shape[0]
    n_tiles = (2 * tt + N_EXPERTS * (MOE_TILE - 1)) // MOE_TILE
    n_tiles_pad = -(-(n_tiles + 1) // SUBLANES) * SUBLANES
    pos, tmeta = _plan(eid, n_tiles_pad)
    code = _invert(pos[2:4].reshape(-1), tt)
    ys = _moe(tmeta, code, hp, wup, wdn, n_tiles, layer)
    return ys, SUBLANES * pos[0:2].reshape(-1)


def kernel(x_prompt, x_sample, p_prompt, p_sample, w_in_a, sink, conv_dw_a, conv_b_a, conv_ln_g, conv_ln_b,
           w_out_a, w_in_c, conv_w_c, w_out_c, ln1_g, ln1_b, ln2_g, ln2_b, w_router_group, b_router_group,
           w_router_expert, b_router_expert, w_up, w_down, w_ple, w_ple_gate, b_ple_gate):
    bp, sp, d = x_prompt.shape
    bs, ss, _ = x_sample.shape
    depth = p_prompt.shape[0]
    tp = bp * sp
    ts = bs * ss
    dims = (tp, sp, ss)
    alpha = (2.0 * depth) ** 0.25

    x_parts = (x_prompt.reshape(tp, d), x_sample.reshape(ts, d))
    tabs = _rope_tables(max(sp, ss))
    pad = LANES - N_GROUPS - N_EXPERTS
    w_route = jnp.concatenate(
        [w_router_group, w_router_expert, jnp.zeros((depth, d, pad), F32)], axis=-1).astype(BF16)
    b_route = jnp.concatenate(
        [b_router_group, b_router_expert, jnp.zeros((depth, pad), F32)], axis=-1).reshape(depth, 1, LANES)

    w_up_all = w_up.reshape((depth * N_EXPERTS,) + w_up.shape[2:])
    w_down_all = w_down.reshape((depth * N_EXPERTS,) + w_down.shape[2:])

    for i in range(depth):
        j = i // 2
        wr, br = w_route[i], b_route[i]
        if i % 2 == 0:
            q, k, v, z = _inproj_a(x_parts, w_in_a[j].astype(BF16), tabs, dims)
            att = _attention(q, k, v, sink[j], dims)
            cv = _conv_a(z, conv_dw_a[j], conv_b_a[j], conv_ln_g[j], conv_ln_b[j], dims)
            h, hp, eid, wt = _outproj_a(att, cv, x_parts, w_out_a[j].astype(BF16), ln1_g[i], ln1_b[i], wr, br, alpha)
        else:
            bg, cx = _inproj_c(x_parts[0], w_in_c[j].astype(BF16))
            h, hp, eid, wt = _outproj_c(bg, cx, conv_w_c[j], x_parts[0], w_out_c[j].astype(BF16), ln1_g[i], ln1_b[i],
                                    wr, br, alpha, dims)
        ys, pos_flat = _moe_block(hp, eid, w_up_all, w_down_all, i)
        p_parts = (p_prompt[i].reshape(tp, -1), p_sample[i].reshape(ts, -1))
        out = _combine(pos_flat, h, ys, wt, p_parts, ln2_g[i], ln2_b[i], w_ple_gate[i].astype(BF16),
                       b_ple_gate[i], w_ple[i].astype(BF16), alpha, tp if i == depth - 1 else None)
        x_parts = (out,)

    y_prompt, y_sample = out
    return y_prompt.reshape(bp, sp, d), y_sample.reshape(bs, ss, d)
```
